```python
import jax, jax.numpy as jnp
from jax import lax
import numpy as np

D_MODEL = 2048
BATCH = 1
SEQ = 8192
DEPTH = 2
DEC_BATCH = 128
DEC_SEQ = 4
PAST_LEN = 2048
PAGE_SIZE = 128

HEAD_DIM = 128
POOL_WIDTH = D_MODEL // 2
POOL_WINDOWS = (2, 4, 8, 16)
POOL_GROUPS = len(POOL_WINDOWS)
POOL_GROUP_DIM = POOL_WIDTH // POOL_GROUPS
POOL_STATE = max(POOL_WINDOWS) - 1
MOBA_WIDTH = D_MODEL - POOL_WIDTH
MOBA_HEADS = MOBA_WIDTH // HEAD_DIM
MOBA_BLOCK = 256
MOBA_TOPK = 3
MOBA_Q_CHUNK = 64
FOX_WIDTH = D_MODEL
FOX_HEADS = FOX_WIDTH // HEAD_DIM
FOX_Q_CHUNK = 128
FORGET_BIAS_INIT = 2.0
EVEN_IN = 2 * POOL_WIDTH + 4 * MOBA_WIDTH
ODD_IN = 4 * FOX_WIDTH + FOX_HEADS
N_EVEN = (DEPTH + 1) // 2
N_ODD = DEPTH // 2
RMS_EPS = 1e-6

kernel_name = 'hybrid_pool_moba_fox_step'


def rmsnorm(x, w):
    xf = x.astype(jnp.float32)
    y = xf * lax.rsqrt(jnp.mean(xf * xf, axis=-1, keepdims=True) + RMS_EPS)
    return (y * w.astype(jnp.float32)).astype(x.dtype)


def alibi_slopes(n):
    return jnp.asarray([2.0 ** (-8.0 * (h + 1) / n) for h in range(n)], dtype=jnp.float32)


def _chunk(n, c):
    return c if n % c == 0 else n


def gather_pages(pool, row):
    g = pool[row]
    return g.reshape((g.shape[0] * g.shape[1],) + g.shape[2:])


def pool_mix(u, u_prev, pos0, w_grp, p_scale):
    B, L, _ = u.shape
    ext = jnp.concatenate([u_prev.astype(u.dtype), u], axis=1)
    cs = jnp.concatenate([jnp.zeros((B, 1, POOL_WIDTH), jnp.float32),
                          jnp.cumsum(ext.astype(jnp.float32), axis=1)], axis=1)
    pos = pos0 + jnp.arange(L)
    lo = POOL_STATE + 1
    means = []
    for g, w in enumerate(POOL_WINDOWS):
        sl = slice(g * POOL_GROUP_DIM, (g + 1) * POOL_GROUP_DIM)
        s = cs[:, lo:lo + L, sl] - cs[:, lo - w:lo - w + L, sl]
        cnt = jnp.minimum(w, pos + 1).astype(jnp.float32)
        means.append(s / cnt[None, :, None])
    d = jnp.concatenate(means, axis=-1) - u.astype(jnp.float32)
    y = jnp.einsum('blgc,gce->blge', d.reshape(B, L, POOL_GROUPS, POOL_GROUP_DIM), w_grp.astype(jnp.float32))
    y = y.reshape(B, L, POOL_WIDTH) * p_scale.astype(jnp.float32)
    return y.astype(u.dtype), ext[:, -POOL_STATE:, :]


def moba_single(q, k, v, q_pos, slopes):
    f32 = jnp.float32
    T = k.shape[0]
    Lq = q.shape[0]
    nblk = -(-T // MOBA_BLOCK)
    pad = nblk * MOBA_BLOCK - T
    kh = jnp.pad(k.astype(f32), ((0, pad), (0, 0), (0, 0))).reshape(nblk, MOBA_BLOCK, MOBA_HEADS, HEAD_DIM).transpose(2, 0, 1, 3)
    vh = jnp.pad(v.astype(f32), ((0, pad), (0, 0), (0, 0))).reshape(nblk, MOBA_BLOCK, MOBA_HEADS, HEAD_DIM).transpose(2, 0, 1, 3)
    kmean = jnp.mean(kh, axis=2)
    ksel = min(MOBA_TOPK, nblk)
    C = _chunk(Lq, MOBA_Q_CHUNK)
    scale = HEAD_DIM ** -0.5
    hidx = jnp.arange(MOBA_HEADS)[None, :, None]
    offs = jnp.arange(MOBA_BLOCK)
    blk_ids = jnp.arange(nblk)

    def block(args):
        qb, pb = args
        qb = qb.astype(f32)
        cur = pb // MOBA_BLOCK
        gate = jnp.einsum('chd,hnd->chn', qb, kmean)
        gate = jnp.where(blk_ids[None, None, :] < cur[:, None, None], gate, -jnp.inf)
        _, top = lax.top_k(gate, ksel)
        own = jnp.broadcast_to(cur[:, None, None], (C, MOBA_HEADS, 1)).astype(top.dtype)
        blocks = jnp.concatenate([top, own], axis=-1)
        slot_ok = jnp.concatenate([jnp.arange(ksel)[None, :] < cur[:, None], jnp.ones((C, 1), bool)], axis=-1)
        gk = kh[hidx, blocks]
        gv = vh[hidx, blocks]
        kpos = blocks[..., None] * MOBA_BLOCK + offs
        dist = (pb[:, None, None, None] - kpos).astype(f32)
        logits = jnp.einsum('chd,chbsd->chbs', qb, gk) * scale - slopes[None, :, None, None] * dist
        mask = slot_ok[:, None, :, None] & (kpos <= pb[:, None, None, None])
        logits = jnp.where(mask, logits, -jnp.inf).reshape(C, MOBA_HEADS, -1)
        p = jax.nn.softmax(logits, axis=-1).reshape(gk.shape[:-1])
        return jnp.einsum('chbs,chbsd->chd', p, gv)

    out = lax.map(block, (q.reshape(Lq // C, C, MOBA_HEADS, HEAD_DIM), q_pos.reshape(Lq // C, C)))
    return out.reshape(Lq, MOBA_HEADS, HEAD_DIM).astype(q.dtype)


def fox_single(q, k, v, lf, q_pos):
    f32 = jnp.float32
    T = k.shape[0]
    Lq = q.shape[0]
    C = _chunk(Lq, FOX_Q_CHUNK)
    c = jnp.cumsum(lf.astype(f32), axis=0)
    cT = c.T
    kf = k.astype(f32)
    vf = v.astype(f32)
    kpos = jnp.arange(T)
    scale = HEAD_DIM ** -0.5

    def block(args):
        qb, pb = args
        cq = c[pb].T
        logits = jnp.einsum('chd,shd->hcs', qb.astype(f32), kf) * scale + cq[:, :, None] - cT[:, None, :]
        mask = kpos[None, None, :] <= pb[None, :, None]
        p = jax.nn.softmax(jnp.where(mask, logits, -jnp.inf), axis=-1)
        return jnp.einsum('hcs,shd->chd', p, vf)

    out = lax.map(block, (q.reshape(Lq // C, C, FOX_HEADS, HEAD_DIM), q_pos.reshape(Lq // C, C)))
    return out.reshape(Lq, FOX_HEADS, HEAD_DIM).astype(q.dtype)


def even_layer(h, u_prev, pos0, attend, norm_w, w_in, w_grp, p_scale, w_out):
    B, L, _ = h.shape
    z = rmsnorm(h, norm_w) @ w_in
    splits = np.cumsum([POOL_WIDTH, POOL_WIDTH, MOBA_WIDTH, MOBA_WIDTH, MOBA_WIDTH]).tolist()
    u, g_pool, q, k, v, g_moba = jnp.split(z, splits, axis=-1)
    pool_out, new_u_state = pool_mix(u, u_prev, pos0, w_grp, p_scale)
    q = q.reshape(B, L, MOBA_HEADS, HEAD_DIM)
    k = k.reshape(B, L, MOBA_HEADS, HEAD_DIM)
    v = v.reshape(B, L, MOBA_HEADS, HEAD_DIM)
    att = attend(q, k, v).reshape(B, L, MOBA_WIDTH)
    mixed = jnp.concatenate([pool_out * jax.nn.silu(g_pool), att * jax.nn.silu(g_moba)], axis=-1)
    return h + mixed @ w_out, new_u_state, k, v


def odd_layer(h, attend, norm_w, w_in, b_f, w_out):
    B, L, _ = h.shape
    z = rmsnorm(h, norm_w) @ w_in
    q, k, v, g, fg = jnp.split(z, [FOX_WIDTH, 2 * FOX_WIDTH, 3 * FOX_WIDTH, 4 * FOX_WIDTH], axis=-1)
    lf = jax.nn.log_sigmoid(fg.astype(jnp.float32) + b_f.astype(jnp.float32))
    q = q.reshape(B, L, FOX_HEADS, HEAD_DIM)
    k = k.reshape(B, L, FOX_HEADS, HEAD_DIM)
    v = v.reshape(B, L, FOX_HEADS, HEAD_DIM)
    att = attend(q, k, v, lf).reshape(B, L, FOX_WIDTH)
    return h + (att * jax.nn.silu(g)) @ w_out, k, v, lf


def setup_inputs(seed: int = 0) -> dict:
    key = jax.random.key(seed)
    ks = jax.random.split(key, 20)
    f32 = jnp.float32
    n_pages = PAST_LEN // PAGE_SIZE
    n_used = DEC_BATCH * n_pages
    n_pool = n_used + (n_used + 3) // 4

    def nrm(k, shape, s=1.0):
        return s * jax.random.normal(k, shape, f32)

    x_prompt = nrm(ks[0], (BATCH, SEQ, D_MODEL))
    x_sample = nrm(ks[1], (DEC_BATCH, DEC_SEQ, D_MODEL))
    state_pool = nrm(ks[2], (N_EVEN, DEC_BATCH, POOL_STATE, POOL_WIDTH))
    cache_moba_k = nrm(ks[3], (N_EVEN, n_pool, PAGE_SIZE, MOBA_HEADS, HEAD_DIM))
    cache_moba_v = nrm(ks[4], (N_EVEN, n_pool, PAGE_SIZE, MOBA_HEADS, HEAD_DIM))
    cache_fox_k = nrm(ks[5], (N_ODD, n_pool, PAGE_SIZE, FOX_HEADS, HEAD_DIM))
    cache_fox_v = nrm(ks[6], (N_ODD, n_pool, PAGE_SIZE, FOX_HEADS, HEAD_DIM))
    cache_fox_logf = jax.nn.log_sigmoid(FORGET_BIAS_INIT + nrm(ks[7], (N_ODD, n_pool, PAGE_SIZE, FOX_HEADS)))
    page_table = jax.random.permutation(ks[8], n_pool)[:n_used].reshape(DEC_BATCH, n_pages).astype(jnp.int32)
    norm_even_w = 1.0 + nrm(ks[9], (N_EVEN, D_MODEL), 0.02)
    w_in_even = nrm(ks[10], (N_EVEN, D_MODEL, EVEN_IN), D_MODEL ** -0.5)
    pool_w = nrm(ks[11], (N_EVEN, POOL_GROUPS, POOL_GROUP_DIM, POOL_GROUP_DIM), POOL_GROUP_DIM ** -0.5)
    pool_scale = 1.0 + nrm(ks[12], (N_EVEN, POOL_WIDTH), 0.02)
    w_out_even = nrm(ks[13], (N_EVEN, POOL_WIDTH + MOBA_WIDTH, D_MODEL), (POOL_WIDTH + MOBA_WIDTH) ** -0.5)
    norm_odd_w = 1.0 + nrm(ks[14], (N_ODD, D_MODEL), 0.02)
    w_in_odd = nrm(ks[15], (N_ODD, D_MODEL, ODD_IN), D_MODEL ** -0.5)
    b_forget = FORGET_BIAS_INIT + nrm(ks[16], (N_ODD, FOX_HEADS), 0.1)
    w_out_odd = nrm(ks[17], (N_ODD, FOX_WIDTH, D_MODEL), FOX_WIDTH ** -0.5)
    norm_final_w = 1.0 + nrm(ks[18], (D_MODEL,), 0.02)
    return {'x_prompt': x_prompt, 'x_sample': x_sample, 'state_pool': state_pool,
            'cache_moba_k': cache_moba_k, 'cache_moba_v': cache_moba_v,
            'cache_fox_k': cache_fox_k, 'cache_fox_v': cache_fox_v, 'cache_fox_logf': cache_fox_logf,
            'page_table': page_table,
            'norm_even_w': norm_even_w, 'w_in_even': w_in_even, 'pool_w': pool_w, 'pool_scale': pool_scale,
            'w_out_even': w_out_even, 'norm_odd_w': norm_odd_w, 'w_in_odd': w_in_odd, 'b_forget': b_forget,
            'w_out_odd': w_out_odd, 'norm_final_w': norm_final_w}


def reference(x_prompt, x_sample, state_pool, cache_moba_k, cache_moba_v, cache_fox_k, cache_fox_v, cache_fox_logf,
              page_table, norm_even_w, w_in_even, pool_w, pool_scale, w_out_even, norm_odd_w, w_in_odd, b_forget,
              w_out_odd, norm_final_w):
    Bp, Lp, _ = x_prompt.shape
    Bs, Ls, _ = x_sample.shape
    pos_p = jnp.arange(Lp, dtype=jnp.int32)
    pos_s = PAST_LEN + jnp.arange(Ls, dtype=jnp.int32)
    slopes = alibi_slopes(MOBA_HEADS)
    hp, hs = x_prompt, x_sample
    pool_p, pool_s, mk_p, mv_p, mk_s, mv_s = [], [], [], [], [], []
    fk_p, fv_p, fl_p, fk_s, fv_s, fl_s = [], [], [], [], [], []
    for layer in range(DEPTH):
        i = layer // 2
        if layer % 2 == 0:
            ck, cv = cache_moba_k[i], cache_moba_v[i]

            def att_p(q, k, v):
                return lax.map(lambda a: moba_single(a[0], a[1], a[2], pos_p, slopes), (q, k, v))

            def att_s(q, k, v):
                def one(a):
                    qb, kb, vb, row = a
                    kall = jnp.concatenate([gather_pages(ck, row), kb], axis=0)
                    vall = jnp.concatenate([gather_pages(cv, row), vb], axis=0)
                    return moba_single(qb, kall, vall, pos_s, slopes)
                return lax.map(one, (q, k, v, page_table))

            zeros_prev = jnp.zeros((Bp, POOL_STATE, POOL_WIDTH), x_prompt.dtype)
            hp, sp, kp, vp = even_layer(hp, zeros_prev, 0, att_p, norm_even_w[i], w_in_even[i], pool_w[i],
                                        pool_scale[i], w_out_even[i])
            hs, ss, ksm, vsm = even_layer(hs, state_pool[i], PAST_LEN, att_s, norm_even_w[i], w_in_even[i], pool_w[i],
                                          pool_scale[i], w_out_even[i])
            pool_p.append(sp); pool_s.append(ss)
            mk_p.append(kp); mv_p.append(vp); mk_s.append(ksm); mv_s.append(vsm)
        else:
            fck, fcv, fcl = cache_fox_k[i], cache_fox_v[i], cache_fox_logf[i]

            def fatt_p(q, k, v, lf):
                return lax.map(lambda a: fox_single(a[0], a[1], a[2], a[3], pos_p), (q, k, v, lf))

            def fatt_s(q, k, v, lf):
                def one(a):
                    qb, kb, vb, lb, row = a
                    kall = jnp.concatenate([gather_pages(fck, row), kb], axis=0)
                    vall = jnp.concatenate([gather_pages(fcv, row), vb], axis=0)
                    lall = jnp.concatenate([gather_pages(fcl, row).astype(jnp.float32), lb], axis=0)
                    return fox_single(qb, kall, vall, lall, pos_s)
                return lax.map(one, (q, k, v, lf, page_table))

            hp, kp, vp, lp = odd_layer(hp, fatt_p, norm_odd_w[i], w_in_odd[i], b_forget[i], w_out_odd[i])
            hs, ksm, vsm, lsm = odd_layer(hs, fatt_s, norm_odd_w[i], w_in_odd[i], b_forget[i], w_out_odd[i])
            fk_p.append(kp); fv_p.append(vp); fl_p.append(lp.astype(cache_fox_logf.dtype))
            fk_s.append(ksm); fv_s.append(vsm); fl_s.append(lsm.astype(cache_fox_logf.dtype))
    y_prompt = rmsnorm(hp, norm_final_w)
    y_sample = rmsnorm(hs, norm_final_w)
    return (y_prompt, y_sample,
            jnp.stack(pool_p), jnp.stack(pool_s),
            jnp.stack(mk_p), jnp.stack(mv_p), jnp.stack(mk_s), jnp.stack(mv_s),
            jnp.stack(fk_p), jnp.stack(fv_p), jnp.stack(fl_p),
            jnp.stack(fk_s), jnp.stack(fv_s), jnp.stack(fl_s))
```

```python
import functools

import jax
import jax.numpy as jnp
from jax import lax
from jax.experimental import pallas as pl
from jax.experimental.pallas import tpu as pltpu

F32 = jnp.float32
BF16 = jnp.bfloat16

HEAD_DIM = 128
POOL_WINDOWS = (2, 4, 8, 16)
POOL_STATE = max(POOL_WINDOWS) - 1
MOBA_BLOCK = 256
MOBA_TOPK = 3
RMS_EPS = 1e-6

V7X_LANES = 128
V7X_SUBLANES = 8
V7X_VMEM_BYTES = 64 * 1024 * 1024
VMEM_LIMIT_BYTES = V7X_VMEM_BYTES - 8 * 1024 * 1024

HEADS_PER_GROUP = 8
GROUP_WIDTH = HEADS_PER_GROUP * HEAD_DIM
NEG_MASK = -1e30
M_INIT = -1e29
HIGHEST = lax.Precision.HIGHEST
NT_DIMS = (((1,), (1,)), ((), ()))


def _params(*semantics):
    return pltpu.CompilerParams(dimension_semantics=semantics, vmem_limit_bytes=VMEM_LIMIT_BYTES)


def _pick_tile(n, preferred):
    t = min(n, preferred)
    while n % t:
        t //= 2
    return t


def _log_sigmoid(x):
    return -(jnp.maximum(-x, 0.0) + jnp.log1p(jnp.exp(-jnp.abs(x))))


def _silu(x):
    return x * jax.nn.sigmoid(x)


def _inproj_kernel(x_ref, nw_ref, w_ref, o_ref, xn_ref, *, row_chunk):
    @pl.when(pl.program_id(1) == 0)
    def _normalize():
        for r in range(0, x_ref.shape[0], row_chunk):
            x = x_ref[r:r + row_chunk, :]
            ms = jnp.mean(x * x, axis=-1, keepdims=True)
            xn_ref[r:r + row_chunk, :] = (x * lax.rsqrt(ms + RMS_EPS) * nw_ref[...]).astype(BF16)

    o_ref[...] = jnp.dot(xn_ref[...], w_ref[...], preferred_element_type=F32)


def _inproj(x, nw, w_bf16, tn):
    m, d = x.shape
    n = w_bf16.shape[1]
    tm = _pick_tile(m, 1024)
    return pl.pallas_call(
        functools.partial(_inproj_kernel, row_chunk=min(tm, 256)),
        out_shape=jax.ShapeDtypeStruct((m, n), F32),
        grid=(m // tm, n // tn),
        in_specs=[pl.BlockSpec((tm, d), lambda i, j: (i, 0)),
                  pl.BlockSpec((1, d), lambda i, j: (0, 0)),
                  pl.BlockSpec((d, tn), lambda i, j: (0, j))],
        out_specs=pl.BlockSpec((tm, tn), lambda i, j: (i, j)),
        scratch_shapes=[pltpu.VMEM((tm, d), BF16)],
        compiler_params=_params("parallel", "arbitrary"),
        name="inproj",
    )(x, nw.reshape(1, d), w_bf16)


def _pool_prompt_kernel(u_ref, prev_ref, w_ref, ps_ref, o_ref, *, halo):
    i = pl.program_id(0)
    tl = u_ref.shape[0]
    gd = w_ref.shape[1]
    cur = u_ref[...]
    prev = jnp.where(i == 0, 0.0, prev_ref[...])
    ext = jnp.concatenate([prev, cur], axis=0)
    pos = i * tl + lax.broadcasted_iota(jnp.int32, (tl, 1), 0)
    for g, w in enumerate(POOL_WINDOWS):
        cols = slice(g * gd, (g + 1) * gd)
        a = ext[:, cols]
        sh = 1
        while sh < w:
            a = a + pltpu.roll(a, sh, axis=0)
            sh *= 2
        cnt = jnp.minimum(w, pos + 1).astype(F32)
        d = a[halo:, :] / cnt - cur[:, cols]
        y = jnp.dot(d.astype(BF16), w_ref[g], preferred_element_type=F32)
        o_ref[:, cols] = y * ps_ref[:, cols]


def _pool_prompt(z, w_grp_bf16, p_scale):
    l = z.shape[0]
    pw = p_scale.shape[-1]
    halo = 16
    tl = _pick_tile(l, 512)
    assert tl % halo == 0 and halo > POOL_STATE
    ng, gd, _ = w_grp_bf16.shape
    return pl.pallas_call(
        functools.partial(_pool_prompt_kernel, halo=halo),
        out_shape=jax.ShapeDtypeStruct((l, pw), F32),
        grid=(l // tl,),
        in_specs=[pl.BlockSpec((tl, pw), lambda i: (i, 0)),
                  pl.BlockSpec((halo, pw), lambda i: (jnp.maximum(i * (tl // halo) - 1, 0), 0)),
                  pl.BlockSpec((ng, gd, gd), lambda i: (0, 0, 0)),
                  pl.BlockSpec((1, pw), lambda i: (0, 0))],
        out_specs=pl.BlockSpec((tl, pw), lambda i: (i, 0)),
        compiler_params=_params("parallel"),
        name="pool_prompt",
    )(z, z, w_grp_bf16, p_scale.reshape(1, pw))


def _pool_sample_kernel(ext_ref, w_ref, ps_ref, o_ref, *, pos0):
    ls = o_ref.shape[0]
    gd = w_ref.shape[1]
    for g, w in enumerate(POOL_WINDOWS):
        cols = slice(g * gd, (g + 1) * gd)
        for i in range(ls):
            row = POOL_STATE + i
            s = ext_ref[row, :, cols]
            for j in range(row - w + 1, row):
                s = s + ext_ref[j, :, cols]
            cnt = float(min(w, pos0 + i + 1))
            d = s / cnt - ext_ref[row, :, cols]
            y = jnp.dot(d.astype(BF16), w_ref[g], preferred_element_type=F32)
            o_ref[i, :, cols] = y * ps_ref[:, cols]


def _pool_sample(ext_t, w_grp_bf16, p_scale, pos0):
    rows, b, pw = ext_t.shape
    ls = rows - POOL_STATE
    return pl.pallas_call(
        functools.partial(_pool_sample_kernel, pos0=pos0),
        out_shape=jax.ShapeDtypeStruct((ls, b, pw), F32),
        compiler_params=pltpu.CompilerParams(vmem_limit_bytes=VMEM_LIMIT_BYTES),
        name="pool_sample",
    )(ext_t, w_grp_bf16, p_scale.reshape(1, pw))


def _kmean_kernel(k_ref, o_ref):
    o_ref[0] = jnp.mean(k_ref[...], axis=0, keepdims=True)


def _kmean(z, col_block, width):
    l = z.shape[0]
    nblk = l // MOBA_BLOCK
    out = pl.pallas_call(
        _kmean_kernel,
        out_shape=jax.ShapeDtypeStruct((nblk, 1, width), F32),
        grid=(nblk,),
        in_specs=[pl.BlockSpec((MOBA_BLOCK, width), lambda n: (n, col_block))],
        out_specs=pl.BlockSpec((1, 1, width), lambda n: (n, 0, 0)),
        compiler_params=_params("parallel"),
        name="moba_kmean",
    )(z)
    return out.reshape(nblk, width)


def _select_topk(gate, cur, topk):
    r, n = gate.shape
    nidx = lax.broadcasted_iota(jnp.int32, (r, n), 1)
    nidx_f = nidx.astype(F32)
    g = jnp.where(nidx < cur, gate, -jnp.inf)
    sel = jnp.zeros((r, n), F32)
    for slot in range(topk):
        mx = jnp.max(g, axis=-1, keepdims=True)
        first = jnp.min(jnp.where(g == mx, nidx_f, float(n)), axis=-1, keepdims=True)
        hit = nidx_f == first
        sel = jnp.where(hit & (cur > slot), 1.0, sel)
        g = jnp.where(hit, -jnp.inf, g)
    return sel


def _tri_tables(nq):
    qi = [q for q in range(nq) for _ in range(q + 1)]
    ki = [k for q in range(nq) for k in range(q + 1)]
    return jnp.asarray(qi, jnp.int32), jnp.asarray(ki, jnp.int32)


def _online_softmax_step(s, v_bf16, m_ref, l_ref, acc_ref):
    m_prev = m_ref[...]
    m_new = jnp.maximum(m_prev, jnp.max(s, axis=-1, keepdims=True))
    alpha = jnp.exp(m_prev - m_new)
    p = jnp.exp(s - m_new)
    l_ref[...] = alpha * l_ref[...] + jnp.sum(p, axis=-1, keepdims=True)
    acc_ref[...] = alpha * acc_ref[...] + jnp.dot(p.astype(BF16), v_bf16, preferred_element_type=F32)
    m_ref[...] = m_new


def _moba_prompt_kernel(qt_ref, kt_ref, slopes_ref, q_ref, k_ref, v_ref, km_ref, o_ref,
                        sel_ref, qs_ref, m_ref, l_ref, acc_ref):
    h = pl.program_id(0)
    t = pl.program_id(1)
    qi = qt_ref[t]
    ki = kt_ref[t]
    tq = q_ref.shape[0]
    tk = k_ref.shape[0]
    nblk = km_ref.shape[0]
    qpos = qi * tq + lax.broadcasted_iota(jnp.int32, (tq, 1), 0)
    cur = qpos // MOBA_BLOCK

    @pl.when(ki == 0)
    def _init():
        q = q_ref[...]
        gate = lax.dot_general(q, km_ref[...], NT_DIMS, precision=HIGHEST, preferred_element_type=F32)
        sel_ref[...] = _select_topk(gate, cur, MOBA_TOPK)
        qs_ref[...] = (q * HEAD_DIM ** -0.5).astype(BF16)
        m_ref[...] = jnp.full(m_ref.shape, M_INIT, F32)
        l_ref[...] = jnp.zeros(l_ref.shape, F32)
        acc_ref[...] = jnp.zeros(acc_ref.shape, F32)

    s = lax.dot_general(qs_ref[...], k_ref[...].astype(BF16), NT_DIMS, preferred_element_type=F32)
    kpos = ki * tk + lax.broadcasted_iota(jnp.int32, (1, tk), 1)
    kblk = kpos // MOBA_BLOCK
    allowed = (kblk == cur) & (kpos <= qpos)
    sel = sel_ref[...]
    nidx = lax.broadcasted_iota(jnp.int32, (tq, nblk), 1)
    for j in range(tk // MOBA_BLOCK):
        b = ki * (tk // MOBA_BLOCK) + j
        picked = jnp.sum(jnp.where(nidx == b, sel, 0.0), axis=-1, keepdims=True) > 0.0
        allowed = allowed | (picked & (kblk == b))
    s = s - slopes_ref[h] * (qpos - kpos).astype(F32)
    s = jnp.where(allowed, s, NEG_MASK)
    _online_softmax_step(s, v_ref[...].astype(BF16), m_ref, l_ref, acc_ref)

    @pl.when(ki == qi)
    def _finish():
        o_ref[...] = acc_ref[...] / l_ref[...]


def _moba_prompt(z, kmean, slopes, q_cb, k_cb, v_cb, n_heads):
    l = z.shape[0]
    assert l % MOBA_BLOCK == 0
    tq = _pick_tile(l, 512)
    assert tq % MOBA_BLOCK == 0
    nblk = kmean.shape[0]
    qt, kt = _tri_tables(l // tq)
    grid_spec = pltpu.PrefetchScalarGridSpec(
        num_scalar_prefetch=2,
        grid=(n_heads, qt.shape[0]),
        in_specs=[pl.BlockSpec(memory_space=pltpu.SMEM),
                  pl.BlockSpec((tq, HEAD_DIM), lambda h, t, qt, kt: (qt[t], q_cb + h)),
                  pl.BlockSpec((tq, HEAD_DIM), lambda h, t, qt, kt: (kt[t], k_cb + h)),
                  pl.BlockSpec((tq, HEAD_DIM), lambda h, t, qt, kt: (kt[t], v_cb + h)),
                  pl.BlockSpec((nblk, HEAD_DIM), lambda h, t, qt, kt: (0, h))],
        out_specs=pl.BlockSpec((tq, HEAD_DIM), lambda h, t, qt, kt: (qt[t], h)),
        scratch_shapes=[pltpu.VMEM((tq, nblk), F32),
                        pltpu.VMEM((tq, HEAD_DIM), BF16),
                        pltpu.VMEM((tq, 1), F32),
                        pltpu.VMEM((tq, 1), F32),
                        pltpu.VMEM((tq, HEAD_DIM), F32)])
    return pl.pallas_call(
        _moba_prompt_kernel,
        out_shape=jax.ShapeDtypeStruct((l, n_heads * HEAD_DIM), F32),
        grid_spec=grid_spec,
        compiler_params=_params("parallel", "arbitrary"),
        name="moba_prompt",
    )(qt, kt, slopes, z, z, z, kmean)


def _logf_cumsum_kernel(fg_ref, b_ref, lf_ref, c_ref, carry_ref):
    tc, nh = lf_ref.shape

    @pl.when(pl.program_id(0) == 0)
    def _init():
        carry_ref[...] = jnp.zeros(carry_ref.shape, F32)

    lf = _log_sigmoid(fg_ref[:, :nh] + b_ref[...])
    lf_ref[...] = lf
    row = lax.broadcasted_iota(jnp.int32, (tc, tc), 0)
    col = lax.broadcasted_iota(jnp.int32, (tc, tc), 1)
    tri = (col <= row).astype(F32)
    c = jnp.dot(tri, lf, precision=HIGHEST, preferred_element_type=F32) + carry_ref[...]
    c_ref[...] = c
    carry_ref[...] = c[tc - 1:tc, :]


def _logf_cumsum(z, fg_cb, b_f):
    l = z.shape[0]
    nh = b_f.shape[-1]
    tc = _pick_tile(l, 512)
    return pl.pallas_call(
        _logf_cumsum_kernel,
        out_shape=(jax.ShapeDtypeStruct((l, nh), F32), jax.ShapeDtypeStruct((l, nh), F32)),
        grid=(l // tc,),
        in_specs=[pl.BlockSpec((tc, V7X_LANES), lambda i: (i, fg_cb)),
                  pl.BlockSpec((1, nh), lambda i: (0, 0))],
        out_specs=(pl.BlockSpec((tc, nh), lambda i: (i, 0)), pl.BlockSpec((tc, nh), lambda i: (i, 0))),
        scratch_shapes=[pltpu.VMEM((1, nh), F32)],
        compiler_params=_params("arbitrary"),
        name="logf_cumsum",
    )(z, b_f.reshape(1, nh))


def _fox_prompt_kernel(qt_ref, kt_ref, q_ref, k_ref, v_ref, c_ref, ct_ref, o_ref,
                       qs_ref, cq_ref, m_ref, l_ref, acc_ref):
    h = pl.program_id(0)
    t = pl.program_id(1)
    qi = qt_ref[t]
    ki = kt_ref[t]
    tq = q_ref.shape[0]
    tk = k_ref.shape[0]

    @pl.when(ki == 0)
    def _init():
        hidx = lax.broadcasted_iota(jnp.int32, c_ref.shape, 1)
        cq_ref[...] = jnp.sum(jnp.where(hidx == h, c_ref[...], 0.0), axis=-1, keepdims=True)
        qs_ref[...] = (q_ref[...] * HEAD_DIM ** -0.5).astype(BF16)
        m_ref[...] = jnp.full(m_ref.shape, M_INIT, F32)
        l_ref[...] = jnp.zeros(l_ref.shape, F32)
        acc_ref[...] = jnp.zeros(acc_ref.shape, F32)

    s = lax.dot_general(qs_ref[...], k_ref[...].astype(BF16), NT_DIMS, preferred_element_type=F32)
    s = s + (cq_ref[...] - ct_ref[0])
    qpos = qi * tq + lax.broadcasted_iota(jnp.int32, (tq, 1), 0)
    kpos = ki * tk + lax.broadcasted_iota(jnp.int32, (1, tk), 1)
    s = jnp.where(kpos <= qpos, s, NEG_MASK)
    _online_softmax_step(s, v_ref[...].astype(BF16), m_ref, l_ref, acc_ref)

    @pl.when(ki == qi)
    def _finish():
        o_ref[...] = acc_ref[...] / l_ref[...]


def _fox_prompt(z, c, q_cb, k_cb, v_cb, n_heads):
    l = z.shape[0]
    tq = _pick_tile(l, 512)
    qt, kt = _tri_tables(l // tq)
    ct = c.T.reshape(n_heads, 1, l)
    grid_spec = pltpu.PrefetchScalarGridSpec(
        num_scalar_prefetch=2,
        grid=(n_heads, qt.shape[0]),
        in_specs=[pl.BlockSpec((tq, HEAD_DIM), lambda h, t, qt, kt: (qt[t], q_cb + h)),
                  pl.BlockSpec((tq, HEAD_DIM), lambda h, t, qt, kt: (kt[t], k_cb + h)),
                  pl.BlockSpec((tq, HEAD_DIM), lambda h, t, qt, kt: (kt[t], v_cb + h)),
                  pl.BlockSpec((tq, n_heads), lambda h, t, qt, kt: (qt[t], 0)),
                  pl.BlockSpec((1, 1, tq), lambda h, t, qt, kt: (h, 0, kt[t]))],
        out_specs=pl.BlockSpec((tq, HEAD_DIM), lambda h, t, qt, kt: (qt[t], h)),
        scratch_shapes=[pltpu.VMEM((tq, HEAD_DIM), BF16),
                        pltpu.VMEM((tq, 1), F32),
                        pltpu.VMEM((tq, 1), F32),
                        pltpu.VMEM((tq, 1), F32),
                        pltpu.VMEM((tq, HEAD_DIM), F32)])
    return pl.pallas_call(
        _fox_prompt_kernel,
        out_shape=jax.ShapeDtypeStruct((l, n_heads * HEAD_DIM), F32),
        grid_spec=grid_spec,
        compiler_params=_params("parallel", "arbitrary"),
        name="fox_prompt",
    )(qt, kt, z, z, z, c, ct)


def _decode_kernel(*refs, fox, layer, n_pages, page, n_groups, chunk):
    if fox:
        (pt_ref, q_ref, kn_ref, vn_ref, fg_ref, bf_ref, kc_hbm, vc_hbm, lfc_hbm,
         o_ref, lf_out_ref, kbuf, vbuf, knbuf, vnbuf, lfbuf, sems) = refs
    else:
        (pt_ref, q_ref, kn_ref, vn_ref, slope_ref, kc_hbm, vc_hbm,
         o_ref, kbuf, vbuf, knbuf, vnbuf, sems) = refs
    b = pl.program_id(0)
    g = pl.program_id(1)
    step = b * n_groups + g
    n_steps = pl.num_programs(0) * n_groups
    slot = step % 2
    ls = q_ref.shape[1]
    nq = ls * HEADS_PER_GROUP
    past = n_pages * page
    n_chunks = past // chunk
    pad_new = V7X_LANES

    def page_copies(bb, gg, sl):
        copies = []
        for j in range(n_pages):
            pg = pt_ref[bb, j]
            rows = pl.ds(j * page, page)
            heads = pl.ds(pl.multiple_of(gg * HEADS_PER_GROUP, HEADS_PER_GROUP), HEADS_PER_GROUP)
            copies.append(pltpu.make_async_copy(kc_hbm.at[layer, pg, :, heads, :], kbuf.at[sl, rows], sems.at[0, sl]))
            copies.append(pltpu.make_async_copy(vc_hbm.at[layer, pg, :, heads, :], vbuf.at[sl, rows], sems.at[1, sl]))
            if fox:
                copies.append(pltpu.make_async_copy(lfc_hbm.at[layer, pg], lfbuf.at[sl, rows, :], sems.at[2, sl]))
        return copies

    def load_wide(buf, ci):
        rows = slice(ci * chunk, (ci + 1) * chunk)
        return jnp.concatenate([buf[slot, rows, h, :] for h in range(HEADS_PER_GROUP)], axis=1)

    @pl.when(step == 0)
    def _first_fetch():
        for c in page_copies(b, g, slot):
            c.start()

    @pl.when(step + 1 < n_steps)
    def _prefetch_next():
        nxt = step + 1
        for c in page_copies(nxt // n_groups, nxt % n_groups, 1 - slot):
            c.start()

    for c in page_copies(b, g, slot):
        c.wait()

    q = q_ref[0]
    ridx = lax.broadcasted_iota(jnp.int32, (nq, 1), 0)
    r_head = ridx % HEADS_PER_GROUP
    r_query = ridx // HEADS_PER_GROUP
    lane_head = lax.broadcasted_iota(jnp.int32, (1, GROUP_WIDTH), 1) // HEAD_DIM
    head_mask = lane_head == r_head
    q_rep = jnp.concatenate(
        [jnp.broadcast_to(q[i:i + 1, :], (HEADS_PER_GROUP, GROUP_WIDTH)) for i in range(ls)], axis=0)
    q_bd = jnp.where(head_mask, q_rep, 0.0)
    qs_bf16 = (q_bd * HEAD_DIM ** -0.5).astype(BF16)

    s_chunks, ksum_chunks = [], []
    for ci in range(n_chunks):
        kc = load_wide(kbuf, ci)
        s_chunks.append(lax.dot_general(qs_bf16, kc.astype(BF16), NT_DIMS, preferred_element_type=F32))
        if not fox:
            ksum_chunks.append(jnp.sum(kc, axis=0, keepdims=True))
    s_c = jnp.concatenate(s_chunks, axis=1)

    knbuf[...] = jnp.zeros(knbuf.shape, F32)
    vnbuf[...] = jnp.zeros(vnbuf.shape, F32)
    knbuf[0:ls, :] = kn_ref[0]
    vnbuf[0:ls, :] = vn_ref[0]
    kn = knbuf[...]
    vn = vnbuf[...]
    s_n = lax.dot_general(qs_bf16, kn.astype(BF16), NT_DIMS, preferred_element_type=F32)
    new_idx = lax.broadcasted_iota(jnp.int32, (1, pad_new), 1)
    allowed_n = new_idx <= r_query
    kpos = lax.broadcasted_iota(jnp.int32, (1, past), 1)

    if fox:
        nh_all = bf_ref.shape[-1]
        lfn = _log_sigmoid(fg_ref[0][:, :nh_all] + bf_ref[...])
        lf_out_ref[0] = lfn
        hcol = lax.broadcasted_iota(jnp.int32, (nq, nh_all), 1)
        expand = (hcol == g * HEADS_PER_GROUP + r_head).astype(F32)
        lfc = lfbuf[slot]
        x = lax.dot_general(expand, lfc, NT_DIMS, precision=HIGHEST, preferred_element_type=F32)
        jr = lax.broadcasted_iota(jnp.int32, (chunk, chunk), 0)
        jc = lax.broadcasted_iota(jnp.int32, (chunk, chunk), 1)
        upper = (jr > jc).astype(F32)
        run = jnp.zeros((nq, 1), F32)
        sfx = [None] * n_chunks
        for ci in reversed(range(n_chunks)):
            xc = x[:, ci * chunk:(ci + 1) * chunk]
            sfx[ci] = jnp.dot(xc, upper, precision=HIGHEST, preferred_element_type=F32) + run
            run = run + jnp.sum(xc, axis=-1, keepdims=True)
        pre = []
        acc_n = jnp.zeros((nq, 1), F32)
        for i in range(ls):
            acc_n = acc_n + jnp.sum(expand * lfn[i:i + 1, :], axis=-1, keepdims=True)
            pre.append(acc_n)
        n_q = jnp.zeros((nq, 1), F32)
        for i in range(ls):
            n_q = jnp.where(r_query == i, pre[i], n_q)
        pre_new = jnp.concatenate(pre + [jnp.zeros((nq, pad_new - ls), F32)], axis=1)
        s_c = s_c + (jnp.concatenate(sfx, axis=1) + n_q)
        s_n = jnp.where(allowed_n, s_n + (n_q - pre_new), NEG_MASK)
    else:
        nblk = past // MOBA_BLOCK
        assert chunk == MOBA_BLOCK
        kmean = jnp.concatenate(ksum_chunks, axis=0) * (1.0 / MOBA_BLOCK)
        gate = lax.dot_general(q_bd, kmean, NT_DIMS, precision=HIGHEST, preferred_element_type=F32)
        cur = jnp.full((nq, 1), nblk, jnp.int32)
        sel = _select_topk(gate, cur, MOBA_TOPK)
        picked = jnp.concatenate(
            [jnp.broadcast_to(sel[:, n:n + 1], (nq, MOBA_BLOCK)) for n in range(nblk)], axis=1) > 0.0
        slope = slope_ref[...]
        qpos = past + r_query
        s_c = jnp.where(picked, s_c - slope * (qpos - kpos).astype(F32), NEG_MASK)
        s_n = jnp.where(allowed_n, s_n - slope * (r_query - new_idx).astype(F32), NEG_MASK)

    m = jnp.maximum(jnp.max(s_c, axis=-1, keepdims=True), jnp.max(s_n, axis=-1, keepdims=True))
    p_c = jnp.exp(s_c - m)
    p_n = jnp.exp(s_n - m)
    denom = jnp.sum(p_c, axis=-1, keepdims=True) + jnp.sum(p_n, axis=-1, keepdims=True)
    p_c = p_c.astype(BF16)
    out = jnp.dot(p_n.astype(BF16), vn.astype(BF16), preferred_element_type=F32)
    for ci in range(n_chunks):
        vc = load_wide(vbuf, ci)
        out = out + jnp.dot(p_c[:, ci * chunk:(ci + 1) * chunk], vc.astype(BF16), preferred_element_type=F32)
    out = jnp.where(head_mask, out / denom, 0.0)
    for i in range(ls):
        rows = slice(i * HEADS_PER_GROUP, (i + 1) * HEADS_PER_GROUP)
        o_ref[0, i:i + 1, :] = jnp.sum(out[rows, :], axis=0, keepdims=True)


def _decode(page_table, z3, q_cb, k_cb, v_cb, k_cache, v_cache, layer, *, slopes=None,
            fg_cb=None, b_f=None, lf_cache=None):
    fox = lf_cache is not None
    bsz, ls, _ = z3.shape
    n_pages = page_table.shape[1]
    page = k_cache.shape[2]
    n_heads = k_cache.shape[3]
    width = n_heads * HEAD_DIM
    n_groups = n_heads // HEADS_PER_GROUP
    past = n_pages * page
    chunk = MOBA_BLOCK
    assert past % MOBA_BLOCK == 0 and ls <= V7X_SUBLANES and n_heads % HEADS_PER_GROUP == 0
    nq = ls * HEADS_PER_GROUP

    def zspec(cb):
        return pl.BlockSpec((1, ls, GROUP_WIDTH), lambda b, g, pt: (b, 0, cb + g))

    in_specs = [zspec(q_cb), zspec(k_cb), zspec(v_cb)]
    args = [z3, z3, z3]
    scratch = [pltpu.VMEM((2, past, HEADS_PER_GROUP, HEAD_DIM), F32),
               pltpu.VMEM((2, past, HEADS_PER_GROUP, HEAD_DIM), F32),
               pltpu.VMEM((V7X_LANES, GROUP_WIDTH), F32), pltpu.VMEM((V7X_LANES, GROUP_WIDTH), F32)]
    out_shape = [jax.ShapeDtypeStruct((bsz, ls, width), F32)]
    out_specs = [pl.BlockSpec((1, ls, GROUP_WIDTH), lambda b, g, pt: (b, 0, g))]
    if fox:
        nh = b_f.shape[-1]
        in_specs += [pl.BlockSpec((1, ls, V7X_LANES), lambda b, g, pt: (b, 0, fg_cb)),
                     pl.BlockSpec((1, nh), lambda b, g, pt: (0, 0))]
        args += [z3, b_f.reshape(1, nh)]
        caches = [k_cache, v_cache, lf_cache]
        scratch.append(pltpu.VMEM((2, past, nh), F32))
        out_shape.append(jax.ShapeDtypeStruct((bsz, ls, nh), F32))
        out_specs.append(pl.BlockSpec((1, ls, nh), lambda b, g, pt: (b, 0, 0)))
    else:
        slope_rows = jnp.tile(slopes, ls).reshape(nq, 1)
        in_specs.append(pl.BlockSpec((nq, 1), lambda b, g, pt: (0, 0)))
        args.append(slope_rows)
        caches = [k_cache, v_cache]
    in_specs += [pl.BlockSpec(memory_space=pl.ANY)] * len(caches)
    args += caches
    scratch.append(pltpu.SemaphoreType.DMA((3, 2)))
    grid_spec = pltpu.PrefetchScalarGridSpec(
        num_scalar_prefetch=1, grid=(bsz, n_groups), in_specs=in_specs, out_specs=out_specs,
        scratch_shapes=scratch)
    return pl.pallas_call(
        functools.partial(_decode_kernel, fox=fox, layer=layer, n_pages=n_pages, page=page, n_groups=n_groups,
                          chunk=chunk),
        out_shape=out_shape,
        grid_spec=grid_spec,
        compiler_params=_params("arbitrary", "arbitrary"),
        name="fox_decode" if fox else "moba_decode",
    )(page_table, *args)


def _outproj_kernel(*refs, n_pairs, final_norm):
    pair_refs = refs[:2 * n_pairs]
    h_ref, w_ref = refs[2 * n_pairs:2 * n_pairs + 2]
    rest = refs[2 * n_pairs + 2:]
    nw_ref, o_ref = (rest[0], rest[1]) if final_norm else (None, rest[0])
    acc = h_ref[...]
    off = 0
    for i in range(n_pairs):
        a_ref, g_ref = pair_refs[2 * i], pair_refs[2 * i + 1]
        wd = a_ref.shape[1]
        mixed = (a_ref[...] * _silu(g_ref[...])).astype(BF16)
        acc = acc + jnp.dot(mixed, w_ref[off:off + wd, :], preferred_element_type=F32)
        off += wd
    if final_norm:
        ms = jnp.mean(acc * acc, axis=-1, keepdims=True)
        acc = acc * lax.rsqrt(ms + RMS_EPS) * nw_ref[...]
    o_ref[...] = acc


def _outproj(pairs, h, w_bf16, final_nw=None):
    m, d = h.shape
    tm = _pick_tile(m, 256)
    in_specs, args = [], []
    for a, (g_arr, g_cb) in pairs:
        wd = a.shape[1]
        in_specs += [pl.BlockSpec((tm, wd), lambda i: (i, 0)),
                     pl.BlockSpec((tm, wd), lambda i, g_cb=g_cb: (i, g_cb))]
        args += [a, g_arr]
    in_specs += [pl.BlockSpec((tm, d), lambda i: (i, 0)),
                 pl.BlockSpec(w_bf16.shape, lambda i: (0, 0))]
    args += [h, w_bf16]
    if final_nw is not None:
        in_specs.append(pl.BlockSpec((1, d), lambda i: (0, 0)))
        args.append(final_nw.reshape(1, d))
    return pl.pallas_call(
        functools.partial(_outproj_kernel, n_pairs=len(pairs), final_norm=final_nw is not None),
        out_shape=jax.ShapeDtypeStruct((m, d), F32),
        grid=(m // tm,),
        in_specs=in_specs,
        out_specs=pl.BlockSpec((tm, d), lambda i: (i, 0)),
        compiler_params=_params("parallel"),
        name="outproj",
    )(*args)


def _pad_cols(w, multiple):
    pad = (-w.shape[1]) % multiple
    return jnp.pad(w, ((0, 0), (0, pad))) if pad else w


def kernel(x_prompt, x_sample, state_pool, cache_moba_k, cache_moba_v, cache_fox_k, cache_fox_v, cache_fox_logf,
           page_table, norm_even_w, w_in_even, pool_w, pool_scale, w_out_even, norm_odd_w, w_in_odd, b_forget,
           w_out_odd, norm_final_w):
    bp, lp, d = x_prompt.shape
    bs, ls, _ = x_sample.shape
    assert bp == 1, "the prompt group is one sequence"
    depth = norm_even_w.shape[0] + norm_odd_w.shape[0]
    pw = pool_scale.shape[-1]
    mw = (w_in_even.shape[-1] - 2 * pw) // 4
    moba_heads = mw // HEAD_DIM
    fw = w_out_odd.shape[1]
    fox_heads = fw // HEAD_DIM
    page = cache_moba_k.shape[2]
    past_len = page_table.shape[1] * page
    assert pw == GROUP_WIDTH and mw == GROUP_WIDTH and fw % GROUP_WIDTH == 0
    slopes = jnp.asarray([2.0 ** (-8.0 * (h + 1) / moba_heads) for h in range(moba_heads)], F32)

    hp = x_prompt.reshape(lp, d)
    hs = x_sample.reshape(bs * ls, d)
    pool_p, pool_s, mk_p, mv_p, mk_s, mv_s = [], [], [], [], [], []
    fk_p, fv_p, fl_p, fk_s, fv_s, fl_s = [], [], [], [], [], []
    for layer in range(depth):
        i = layer // 2
        last = layer == depth - 1
        if layer % 2 == 0:
            w_in = w_in_even[i].astype(BF16)
            w_grp = pool_w[i].astype(BF16)
            w_out = w_out_even[i].astype(BF16)
            cb = GROUP_WIDTH // HEAD_DIM
            zp = _inproj(hp, norm_even_w[i], w_in, GROUP_WIDTH)
            zs = _inproj(hs, norm_even_w[i], w_in, GROUP_WIDTH)
            pool_out_p = _pool_prompt(zp, w_grp, pool_scale[i])
            kmean = _kmean(zp, 3, GROUP_WIDTH)
            att_p = _moba_prompt(zp, kmean, slopes, 2 * cb, 3 * cb, 4 * cb, moba_heads)
            zs3 = zs.reshape(bs, ls, -1)
            u_s = zs3[:, :, :pw]
            ext_s = jnp.concatenate([state_pool[i], u_s], axis=1)
            pool_out_s = _pool_sample(ext_s.transpose(1, 0, 2), w_grp, pool_scale[i], past_len)
            pool_out_s = pool_out_s.transpose(1, 0, 2).reshape(bs * ls, pw)
            (att_s,) = _decode(page_table, zs3, 2, 3, 4, cache_moba_k, cache_moba_v, i, slopes=slopes)
            att_s = att_s.reshape(bs * ls, mw)
            u_p = zp[:, :pw]
            ext_p = jnp.concatenate([jnp.zeros((POOL_STATE, pw), F32), u_p[-POOL_STATE:]], axis=0)
            pool_p.append(ext_p[-POOL_STATE:][None])
            pool_s.append(ext_s[:, -POOL_STATE:])
            mk_p.append(zp[:, 3 * pw:4 * pw].reshape(1, lp, moba_heads, HEAD_DIM))
            mv_p.append(zp[:, 4 * pw:5 * pw].reshape(1, lp, moba_heads, HEAD_DIM))
            mk_s.append(zs[:, 3 * pw:4 * pw].reshape(bs, ls, moba_heads, HEAD_DIM))
            mv_s.append(zs[:, 4 * pw:5 * pw].reshape(bs, ls, moba_heads, HEAD_DIM))
            nw = norm_final_w if last else None
            hp = _outproj([(pool_out_p, (zp, 1)), (att_p, (zp, 5))], hp, w_out, nw)
            hs = _outproj([(pool_out_s, (zs, 1)), (att_s, (zs, 5))], hs, w_out, nw)
        else:
            w_in = _pad_cols(w_in_odd[i], 5 * V7X_LANES).astype(BF16)
            w_out = w_out_odd[i].astype(BF16)
            tn = 5 * V7X_LANES
            zp = _inproj(hp, norm_odd_w[i], w_in, tn)
            zs = _inproj(hs, norm_odd_w[i], w_in, tn)
            fg_cb = 4 * fw // V7X_LANES
            cb = fw // HEAD_DIM
            lf_p, c_p = _logf_cumsum(zp, fg_cb, b_forget[i])
            att_p = _fox_prompt(zp, c_p, 0, cb, 2 * cb, fox_heads)
            zs3 = zs.reshape(bs, ls, -1)
            gpb = fw // GROUP_WIDTH
            att_s, lf_s = _decode(page_table, zs3, 0, gpb, 2 * gpb, cache_fox_k, cache_fox_v, i,
                                  fg_cb=fg_cb, b_f=b_forget[i], lf_cache=cache_fox_logf)
            att_s = att_s.reshape(bs * ls, fw)
            fk_p.append(zp[:, fw:2 * fw].reshape(1, lp, fox_heads, HEAD_DIM))
            fv_p.append(zp[:, 2 * fw:3 * fw].reshape(1, lp, fox_heads, HEAD_DIM))
            fl_p.append(lf_p.reshape(1, lp, fox_heads).astype(cache_fox_logf.dtype))
            fk_s.append(zs[:, fw:2 * fw].reshape(bs, ls, fox_heads, HEAD_DIM))
            fv_s.append(zs[:, 2 * fw:3 * fw].reshape(bs, ls, fox_heads, HEAD_DIM))
            fl_s.append(lf_s.astype(cache_fox_logf.dtype))
            nw = norm_final_w if last else None
            hp = _outproj([(att_p, (zp, 3))], hp, w_out, nw)
            hs = _outproj([(att_s, (zs, 3))], hs, w_out, nw)
    y_prompt = hp.reshape(bp, lp, d)
    y_sample = hs.reshape(bs, ls, d)
    return (y_prompt, y_sample,
            jnp.stack(pool_p), jnp.stack(pool_s),
            jnp.stack(mk_p), jnp.stack(mv_p), jnp.stack(mk_s), jnp.stack(mv_s),
            jnp.stack(fk_p), jnp.stack(fv_p), jnp.stack(fl_p),
            jnp.stack(fk_s), jnp.stack(fv_s), jnp.stack(fl_s))
```

```python
import functools

import jax
import jax.numpy as jnp
from jax import lax
from jax.experimental import pallas as pl
from jax.experimental.pallas import tpu as pltpu

F32 = jnp.float32
BF16 = jnp.bfloat16

HEAD_DIM = 128
POOL_WINDOWS = (2, 4, 8, 16)
POOL_STATE = max(POOL_WINDOWS) - 1
MOBA_BLOCK = 256
MOBA_TOPK = 3
RMS_EPS = 1e-6

V7X_LANES = 128
V7X_SUBLANES = 8
V7X_VMEM_BYTES = 64 * 1024 * 1024
VMEM_LIMIT_BYTES = V7X_VMEM_BYTES - 8 * 1024 * 1024

HEADS_PER_GROUP = 8
GROUP_WIDTH = HEADS_PER_GROUP * HEAD_DIM
AUX_WIDTH = V7X_LANES
LOG2E = 1.4426950408889634
NEG_MASK = -1e30
NEG_BIG = -2.0 ** 100
M_INIT = -2.0 ** 99
HIGHEST = lax.Precision.HIGHEST
NT_DIMS = (((1,), (1,)), ((), ()))


def _params(*semantics):
    return pltpu.CompilerParams(dimension_semantics=semantics, vmem_limit_bytes=VMEM_LIMIT_BYTES)


def _pick_tile(n, preferred):
    t = min(n, preferred)
    while n % t:
        t //= 2
    return t


def _log_sigmoid(x):
    return -(jnp.maximum(-x, 0.0) + jnp.log1p(jnp.exp(-jnp.abs(x))))


def _silu(x):
    return x * jax.nn.sigmoid(x)


def _inproj_kernel(x_ref, nw_ref, w_ref, o_ref, xn_ref, *, row_chunk):
    @pl.when(pl.program_id(1) == 0)
    def _normalize():
        for r in range(0, x_ref.shape[0], row_chunk):
            x = x_ref[r:r + row_chunk, :]
            ms = jnp.mean(x * x, axis=-1, keepdims=True)
            xn_ref[r:r + row_chunk, :] = (x * lax.rsqrt(ms + RMS_EPS) * nw_ref[...]).astype(BF16)

    o_ref[...] = jnp.dot(xn_ref[...], w_ref[...], preferred_element_type=F32)


def _inproj(x, nw, w_bf16, tn):
    m, d = x.shape
    n = w_bf16.shape[1]
    tm = _pick_tile(m, 1024)
    return pl.pallas_call(
        functools.partial(_inproj_kernel, row_chunk=min(tm, 256)),
        out_shape=jax.ShapeDtypeStruct((m, n), F32),
        grid=(m // tm, n // tn),
        in_specs=[pl.BlockSpec((tm, d), lambda i, j: (i, 0)),
                  pl.BlockSpec((1, d), lambda i, j: (0, 0)),
                  pl.BlockSpec((d, tn), lambda i, j: (0, j))],
        out_specs=pl.BlockSpec((tm, tn), lambda i, j: (i, j)),
        scratch_shapes=[pltpu.VMEM((tm, d), BF16)],
        compiler_params=_params("parallel", "arbitrary"),
        name="inproj",
    )(x, nw.reshape(1, d), w_bf16)


def _pool_prompt_kernel(u_ref, prev_ref, w_ref, ps_ref, o_ref, *, halo):
    i = pl.program_id(0)
    tl = u_ref.shape[0]
    gd = w_ref.shape[1]
    cur = u_ref[...]
    prev = jnp.where(i == 0, 0.0, prev_ref[...])
    ext = jnp.concatenate([prev, cur], axis=0)
    pos = i * tl + lax.broadcasted_iota(jnp.int32, (tl, 1), 0)
    for g, w in enumerate(POOL_WINDOWS):
        cols = slice(g * gd, (g + 1) * gd)
        a = ext[:, cols]
        sh = 1
        while sh < w:
            a = a + pltpu.roll(a, sh, axis=0)
            sh *= 2
        cnt = jnp.minimum(w, pos + 1).astype(F32)
        d = a[halo:, :] / cnt - cur[:, cols]
        y = jnp.dot(d.astype(BF16), w_ref[g], preferred_element_type=F32)
        o_ref[:, cols] = y * ps_ref[:, cols]


def _pool_prompt(z, w_grp_bf16, p_scale):
    l = z.shape[0]
    pw = p_scale.shape[-1]
    halo = 16
    tl = _pick_tile(l, 512)
    assert tl % halo == 0 and halo > POOL_STATE
    ng, gd, _ = w_grp_bf16.shape
    return pl.pallas_call(
        functools.partial(_pool_prompt_kernel, halo=halo),
        out_shape=jax.ShapeDtypeStruct((l, pw), F32),
        grid=(l // tl,),
        in_specs=[pl.BlockSpec((tl, pw), lambda i: (i, 0)),
                  pl.BlockSpec((halo, pw), lambda i: (jnp.maximum(i * (tl // halo) - 1, 0), 0)),
                  pl.BlockSpec((ng, gd, gd), lambda i: (0, 0, 0)),
                  pl.BlockSpec((1, pw), lambda i: (0, 0))],
        out_specs=pl.BlockSpec((tl, pw), lambda i: (i, 0)),
        compiler_params=_params("parallel"),
        name="pool_prompt",
    )(z, z, w_grp_bf16, p_scale.reshape(1, pw))


def _pool_sample_kernel(ext_ref, w_ref, ps_ref, o_ref, *, pos0):
    ls = o_ref.shape[0]
    gd = w_ref.shape[1]
    for g, w in enumerate(POOL_WINDOWS):
        cols = slice(g * gd, (g + 1) * gd)
        for i in range(ls):
            row = POOL_STATE + i
            s = ext_ref[row, :, cols]
            for j in range(row - w + 1, row):
                s = s + ext_ref[j, :, cols]
            cnt = float(min(w, pos0 + i + 1))
            d = s / cnt - ext_ref[row, :, cols]
            y = jnp.dot(d.astype(BF16), w_ref[g], preferred_element_type=F32)
            o_ref[i, :, cols] = y * ps_ref[:, cols]


def _pool_sample(ext_t, w_grp_bf16, p_scale, pos0):
    rows, b, pw = ext_t.shape
    ls = rows - POOL_STATE
    return pl.pallas_call(
        functools.partial(_pool_sample_kernel, pos0=pos0),
        out_shape=jax.ShapeDtypeStruct((ls, b, pw), F32),
        compiler_params=pltpu.CompilerParams(vmem_limit_bytes=VMEM_LIMIT_BYTES),
        name="pool_sample",
    )(ext_t, w_grp_bf16, p_scale.reshape(1, pw))


def _kmean_kernel(k_ref, o_ref):
    o_ref[0] = jnp.mean(k_ref[...], axis=0, keepdims=True)


def _kmean(z, col_block, width):
    l = z.shape[0]
    nblk = l // MOBA_BLOCK
    out = pl.pallas_call(
        _kmean_kernel,
        out_shape=jax.ShapeDtypeStruct((nblk, 1, width), F32),
        grid=(nblk,),
        in_specs=[pl.BlockSpec((MOBA_BLOCK, width), lambda n: (n, col_block))],
        out_specs=pl.BlockSpec((1, 1, width), lambda n: (n, 0, 0)),
        compiler_params=_params("parallel"),
        name="moba_kmean",
    )(z)
    return out.reshape(nblk, width)


def _select_topk(gate, cur, topk):
    r, n = gate.shape
    nidx = lax.broadcasted_iota(jnp.int32, (r, n), 1)
    nidx_f = nidx.astype(F32)
    g = jnp.where(nidx < cur, gate, -jnp.inf)
    sel = jnp.zeros((r, n), F32)
    for slot in range(topk):
        mx = jnp.max(g, axis=-1, keepdims=True)
        first = jnp.min(jnp.where(g == mx, nidx_f, float(n)), axis=-1, keepdims=True)
        hit = nidx_f == first
        sel = jnp.where(hit & (cur > slot), 1.0, sel)
        g = jnp.where(hit, -jnp.inf, g)
    return sel


def _split3(x):
    hi = x.astype(BF16).astype(F32)
    rest = x - hi
    mid = rest.astype(BF16).astype(F32)
    lo = (rest - mid).astype(BF16).astype(F32)
    return hi, mid, lo


def _attn_prep_kernel(*refs, moba, n_heads, nblk):
    if moba:
        slopes_ref, q_ref, k_ref, v_ref, km_ref, qp_ref, kp_ref, vp_ref = refs
    else:
        q_ref, k_ref, v_ref, c_ref, qp_ref, kp_ref, vp_ref = refs
    tp = q_ref.shape[0]
    pos = pl.program_id(0) * tp + lax.broadcasted_iota(jnp.int32, (tp, 1), 0)
    lane = lax.broadcasted_iota(jnp.int32, (tp, AUX_WIDTH), 1)
    nb = nblk if moba else 0
    ones3 = jnp.where((lane >= nb) & (lane < nb + 3), 1.0, 0.0)
    vaux = jnp.where(lane == 0, 1.0, 0.0).astype(BF16)
    qscale = HEAD_DIM ** -0.5 * LOG2E
    for h in range(n_heads):
        cols = slice(h * HEAD_DIM, (h + 1) * HEAD_DIM)
        q = q_ref[:, cols]
        if moba:
            cur = pos // MOBA_BLOCK
            gate = lax.dot_general(q, km_ref[:, cols], NT_DIMS, precision=HIGHEST, preferred_element_type=F32)
            sel = _select_topk(gate, cur, MOBA_TOPK)
            visible = (sel > 0.0) | (lane == cur)
            qaux = jnp.where(lane < nb, jnp.where(visible, 0.0, NEG_BIG), ones3)
            bias = (slopes_ref[h] * LOG2E) * pos.astype(F32)
            onehot = jnp.where(lane == cur, 1.0, 0.0)
        else:
            hid = lax.broadcasted_iota(jnp.int32, c_ref.shape, 1)
            bias = -LOG2E * jnp.sum(jnp.where(hid == h, c_ref[...], 0.0), axis=-1, keepdims=True)
            qaux = ones3
            onehot = jnp.zeros((tp, AUX_WIDTH), F32)
        hi, mid, lo = _split3(bias)
        kaux = jnp.where(lane == nb, hi, jnp.where(lane == nb + 1, mid, jnp.where(lane == nb + 2, lo, onehot)))
        qp_ref[h] = jnp.concatenate([(q * qscale).astype(BF16), qaux.astype(BF16)], axis=1)
        kp_ref[h] = jnp.concatenate([k_ref[:, cols].astype(BF16), kaux.astype(BF16)], axis=1)
        vp_ref[h] = jnp.concatenate([v_ref[:, cols].astype(BF16), vaux], axis=1)


def _attn_prep(z, q_cb, k_cb, v_cb, n_heads, *, slopes=None, kmean=None, c=None):
    moba = kmean is not None
    l = z.shape[0]
    width = n_heads * HEAD_DIM
    tp = _pick_tile(l, 256)
    nblk = l // MOBA_BLOCK
    zspec = lambda cb: pl.BlockSpec((tp, width), lambda i: (i, cb))
    in_specs = [zspec(q_cb), zspec(k_cb), zspec(v_cb)]
    args = [z, z, z]
    if moba:
        assert l % MOBA_BLOCK == 0 and nblk + 3 <= AUX_WIDTH
        km = jnp.pad(kmean, ((0, AUX_WIDTH - nblk), (0, 0)))
        in_specs = [pl.BlockSpec(memory_space=pltpu.SMEM)] + in_specs + [pl.BlockSpec(km.shape, lambda i: (0, 0))]
        args = [slopes] + args + [km]
    else:
        in_specs.append(pl.BlockSpec((tp, c.shape[1]), lambda i: (i, 0)))
        args.append(c)
    out = jax.ShapeDtypeStruct((n_heads, l, HEAD_DIM + AUX_WIDTH), BF16)
    ospec = pl.BlockSpec((n_heads, tp, HEAD_DIM + AUX_WIDTH), lambda i: (0, i, 0))
    return pl.pallas_call(
        functools.partial(_attn_prep_kernel, moba=moba, n_heads=n_heads, nblk=nblk),
        out_shape=(out, out, out),
        grid=(l // tp,),
        in_specs=in_specs,
        out_specs=(ospec, ospec, ospec),
        compiler_params=_params("parallel"),
        name="moba_prep" if moba else "fox_prep",
    )(*args)


def _flash_kernel(qp_ref, kp_ref, vp_ref, o_ref, m_ref, acc_ref, *, n_split):
    qi = pl.program_id(1)
    tq = qp_ref.shape[1]
    th = tq // n_split
    m_ref[...] = jnp.full(m_ref.shape, M_INIT, F32)
    acc_ref[...] = jnp.zeros(acc_ref.shape, F32)

    def chunk(ki, diagonal):
        start = pl.multiple_of(ki * tq, tq)
        logits = []
        for a in range(n_split):
            n_keys = (a + 1) * th if diagonal else tq
            q = qp_ref[0, a * th:(a + 1) * th, :]
            s = lax.dot_general(q, kp_ref[0, pl.ds(start, n_keys), :], NT_DIMS, preferred_element_type=F32)
            if diagonal:
                row = lax.broadcasted_iota(jnp.int32, (th, n_keys), 0) + a * th
                col = lax.broadcasted_iota(jnp.int32, (th, n_keys), 1)
                s = jnp.where(col <= row, s, NEG_BIG)
            logits.append(s)
        probs, alphas = [], []
        for a in range(n_split):
            rows = slice(a * th, (a + 1) * th)
            m_prev = m_ref[rows, :]
            m_new = jnp.maximum(m_prev, jnp.max(logits[a], axis=-1, keepdims=True))
            alphas.append(jnp.exp2(m_prev - m_new))
            probs.append(jnp.exp2(logits[a] - m_new).astype(BF16))
            m_ref[rows, :] = m_new
        for a in range(n_split):
            rows = slice(a * th, (a + 1) * th)
            n_keys = probs[a].shape[1]
            pv = jnp.dot(probs[a], vp_ref[0, pl.ds(start, n_keys), :], preferred_element_type=F32)
            acc_ref[rows, :] = alphas[a] * acc_ref[rows, :] + pv

    def off_diagonal(ki, carry):
        chunk(ki, False)
        return carry

    lax.fori_loop(0, qi, off_diagonal, None)
    chunk(qi, True)
    acc = acc_ref[...]
    o_ref[...] = acc[:, :HEAD_DIM] / acc[:, HEAD_DIM:HEAD_DIM + 1]


def _flash(qp, kp, vp):
    n_heads, l, aug = qp.shape
    tq = _pick_tile(l, 1024)
    n_split = max(1, tq // 256)
    head_block = pl.BlockSpec((1, l, aug), lambda h, i: (h, 0, 0))
    return pl.pallas_call(
        functools.partial(_flash_kernel, n_split=n_split),
        out_shape=jax.ShapeDtypeStruct((l, n_heads * HEAD_DIM), F32),
        grid=(n_heads, l // tq),
        in_specs=[pl.BlockSpec((1, tq, aug), lambda h, i: (h, i, 0)), head_block, head_block],
        out_specs=pl.BlockSpec((tq, HEAD_DIM), lambda h, i: (i, h)),
        scratch_shapes=[pltpu.VMEM((tq, 1), F32), pltpu.VMEM((tq, aug), F32)],
        compiler_params=_params("parallel", "arbitrary"),
        name="flash",
    )(qp, kp, vp)


def _logf_cumsum_kernel(fg_ref, b_ref, lf_ref, c_ref, carry_ref):
    tc, nh = lf_ref.shape

    @pl.when(pl.program_id(0) == 0)
    def _init():
        carry_ref[...] = jnp.zeros(carry_ref.shape, F32)

    lf = _log_sigmoid(fg_ref[:, :nh] + b_ref[...])
    lf_ref[...] = lf
    row = lax.broadcasted_iota(jnp.int32, (tc, tc), 0)
    col = lax.broadcasted_iota(jnp.int32, (tc, tc), 1)
    tri = (col <= row).astype(F32)
    c = jnp.dot(tri, lf, precision=HIGHEST, preferred_element_type=F32) + carry_ref[...]
    c_ref[...] = c
    carry_ref[...] = c[tc - 1:tc, :]


def _logf_cumsum(z, fg_cb, b_f):
    l = z.shape[0]
    nh = b_f.shape[-1]
    tc = _pick_tile(l, 512)
    return pl.pallas_call(
        _logf_cumsum_kernel,
        out_shape=(jax.ShapeDtypeStruct((l, nh), F32), jax.ShapeDtypeStruct((l, nh), F32)),
        grid=(l // tc,),
        in_specs=[pl.BlockSpec((tc, V7X_LANES), lambda i: (i, fg_cb)),
                  pl.BlockSpec((1, nh), lambda i: (0, 0))],
        out_specs=(pl.BlockSpec((tc, nh), lambda i: (i, 0)), pl.BlockSpec((tc, nh), lambda i: (i, 0))),
        scratch_shapes=[pltpu.VMEM((1, nh), F32)],
        compiler_params=_params("arbitrary"),
        name="logf_cumsum",
    )(z, b_f.reshape(1, nh))


def _decode_kernel(*refs, fox, layer, n_pages, page, n_groups, chunk):
    if fox:
        (pt_ref, q_ref, kn_ref, vn_ref, fg_ref, bf_ref, kc_hbm, vc_hbm, lfc_hbm,
         o_ref, lf_out_ref, kbuf, vbuf, knbuf, vnbuf, lfbuf, sems) = refs
    else:
        (pt_ref, q_ref, kn_ref, vn_ref, slope_ref, kc_hbm, vc_hbm,
         o_ref, kbuf, vbuf, knbuf, vnbuf, sems) = refs
    b = pl.program_id(0)
    g = pl.program_id(1)
    step = b * n_groups + g
    n_steps = pl.num_programs(0) * n_groups
    slot = step % 2
    ls = q_ref.shape[1]
    nq = ls * HEADS_PER_GROUP
    past = n_pages * page
    n_chunks = past // chunk
    pad_new = V7X_LANES

    def page_copies(bb, gg, sl):
        copies = []
        for j in range(n_pages):
            pg = pt_ref[bb, j]
            rows = pl.ds(j * page, page)
            heads = pl.ds(pl.multiple_of(gg * HEADS_PER_GROUP, HEADS_PER_GROUP), HEADS_PER_GROUP)
            copies.append(pltpu.make_async_copy(kc_hbm.at[layer, pg, :, heads, :], kbuf.at[sl, rows], sems.at[0, sl]))
            copies.append(pltpu.make_async_copy(vc_hbm.at[layer, pg, :, heads, :], vbuf.at[sl, rows], sems.at[1, sl]))
            if fox:
                copies.append(pltpu.make_async_copy(lfc_hbm.at[layer, pg], lfbuf.at[sl, j], sems.at[2, sl]))
        return copies

    def load_wide(buf, ci):
        flat = buf.reshape(2 * past * HEADS_PER_GROUP, HEAD_DIM)
        base = (slot * past + ci * chunk) * HEADS_PER_GROUP
        return jnp.concatenate(
            [flat[pl.ds(base + h, chunk, stride=HEADS_PER_GROUP), :] for h in range(HEADS_PER_GROUP)], axis=1)

    @pl.when(step == 0)
    def _first_fetch():
        for c in page_copies(b, g, slot):
            c.start()

    @pl.when(step + 1 < n_steps)
    def _prefetch_next():
        nxt = step + 1
        for c in page_copies(nxt // n_groups, nxt % n_groups, 1 - slot):
            c.start()

    for c in page_copies(b, g, slot):
        c.wait()

    q = q_ref[0]
    ridx = lax.broadcasted_iota(jnp.int32, (nq, 1), 0)
    r_head = ridx % HEADS_PER_GROUP
    r_query = ridx // HEADS_PER_GROUP
    lane_head = lax.broadcasted_iota(jnp.int32, (1, GROUP_WIDTH), 1) // HEAD_DIM
    head_mask = lane_head == r_head
    q_rep = jnp.concatenate(
        [jnp.broadcast_to(q[i:i + 1, :], (HEADS_PER_GROUP, GROUP_WIDTH)) for i in range(ls)], axis=0)
    q_bd = jnp.where(head_mask, q_rep, 0.0)
    qs_bf16 = (q_bd * HEAD_DIM ** -0.5).astype(BF16)

    s_chunks, ksum_chunks = [], []
    for ci in range(n_chunks):
        kc = load_wide(kbuf, ci)
        s_chunks.append(lax.dot_general(qs_bf16, kc.astype(BF16), NT_DIMS, preferred_element_type=F32))
        if not fox:
            ksum_chunks.append(jnp.sum(kc, axis=0, keepdims=True))
    s_c = jnp.concatenate(s_chunks, axis=1)

    knbuf[...] = jnp.zeros(knbuf.shape, F32)
    vnbuf[...] = jnp.zeros(vnbuf.shape, F32)
    knbuf[0:ls, :] = kn_ref[0]
    vnbuf[0:ls, :] = vn_ref[0]
    kn = knbuf[...]
    vn = vnbuf[...]
    s_n = lax.dot_general(qs_bf16, kn.astype(BF16), NT_DIMS, preferred_element_type=F32)
    new_idx = lax.broadcasted_iota(jnp.int32, (1, pad_new), 1)
    allowed_n = new_idx <= r_query
    kpos = lax.broadcasted_iota(jnp.int32, (1, past), 1)

    if fox:
        nh_all = bf_ref.shape[-1]
        lfn = _log_sigmoid(fg_ref[0][:, :nh_all] + bf_ref[...])
        lf_out_ref[0] = lfn
        hcol = lax.broadcasted_iota(jnp.int32, (nq, nh_all), 1)
        expand = (hcol == g * HEADS_PER_GROUP + r_head).astype(F32)
        lfc = jnp.concatenate([lfbuf[slot, j] for j in range(n_pages)], axis=1)
        x = jnp.dot(expand, lfc, precision=HIGHEST, preferred_element_type=F32)
        jr = lax.broadcasted_iota(jnp.int32, (chunk, chunk), 0)
        jc = lax.broadcasted_iota(jnp.int32, (chunk, chunk), 1)
        upper = (jr > jc).astype(F32)
        run = jnp.zeros((nq, 1), F32)
        sfx = [None] * n_chunks
        for ci in reversed(range(n_chunks)):
            xc = x[:, ci * chunk:(ci + 1) * chunk]
            sfx[ci] = jnp.dot(xc, upper, precision=HIGHEST, preferred_element_type=F32) + run
            run = run + jnp.sum(xc, axis=-1, keepdims=True)
        pre = []
        acc_n = jnp.zeros((nq, 1), F32)
        for i in range(ls):
            acc_n = acc_n + jnp.sum(expand * lfn[i:i + 1, :], axis=-1, keepdims=True)
            pre.append(acc_n)
        n_q = jnp.zeros((nq, 1), F32)
        for i in range(ls):
            n_q = jnp.where(r_query == i, pre[i], n_q)
        pre_new = jnp.concatenate(pre + [jnp.zeros((nq, pad_new - ls), F32)], axis=1)
        s_c = s_c + (jnp.concatenate(sfx, axis=1) + n_q)
        s_n = jnp.where(allowed_n, s_n + (n_q - pre_new), NEG_MASK)
    else:
        nblk = past // MOBA_BLOCK
        assert chunk == MOBA_BLOCK
        kmean = jnp.concatenate(ksum_chunks, axis=0) * (1.0 / MOBA_BLOCK)
        gate = lax.dot_general(q_bd, kmean, NT_DIMS, precision=HIGHEST, preferred_element_type=F32)
        cur = jnp.full((nq, 1), nblk, jnp.int32)
        sel = _select_topk(gate, cur, MOBA_TOPK)
        picked = jnp.concatenate(
            [jnp.broadcast_to(sel[:, n:n + 1], (nq, MOBA_BLOCK)) for n in range(nblk)], axis=1) > 0.0
        slope = slope_ref[...]
        qpos = past + r_query
        s_c = jnp.where(picked, s_c - slope * (qpos - kpos).astype(F32), NEG_MASK)
        s_n = jnp.where(allowed_n, s_n - slope * (r_query - new_idx).astype(F32), NEG_MASK)

    m = jnp.maximum(jnp.max(s_c, axis=-1, keepdims=True), jnp.max(s_n, axis=-1, keepdims=True))
    p_c = jnp.exp(s_c - m)
    p_n = jnp.exp(s_n - m)
    denom = jnp.sum(p_c, axis=-1, keepdims=True) + jnp.sum(p_n, axis=-1, keepdims=True)
    p_c = p_c.astype(BF16)
    out = jnp.dot(p_n.astype(BF16), vn.astype(BF16), preferred_element_type=F32)
    for ci in range(n_chunks):
        vc = load_wide(vbuf, ci)
        out = out + jnp.dot(p_c[:, ci * chunk:(ci + 1) * chunk], vc.astype(BF16), preferred_element_type=F32)
    out = jnp.where(head_mask, out / denom, 0.0)
    for i in range(ls):
        rows = slice(i * HEADS_PER_GROUP, (i + 1) * HEADS_PER_GROUP)
        o_ref[0, i:i + 1, :] = jnp.sum(out[rows, :], axis=0, keepdims=True)


def _decode(page_table, z3, q_cb, k_cb, v_cb, k_cache, v_cache, layer, *, slopes=None,
            fg_cb=None, b_f=None, lf_cache=None):
    fox = lf_cache is not None
    bsz, ls, _ = z3.shape
    n_pages = page_table.shape[1]
    page = k_cache.shape[2]
    n_heads = k_cache.shape[3]
    width = n_heads * HEAD_DIM
    n_groups = n_heads // HEADS_PER_GROUP
    past = n_pages * page
    chunk = MOBA_BLOCK
    assert past % MOBA_BLOCK == 0 and ls <= V7X_SUBLANES and n_heads % HEADS_PER_GROUP == 0
    nq = ls * HEADS_PER_GROUP

    def zspec(cb):
        return pl.BlockSpec((1, ls, GROUP_WIDTH), lambda b, g, pt: (b, 0, cb + g))

    in_specs = [zspec(q_cb), zspec(k_cb), zspec(v_cb)]
    args = [z3, z3, z3]
    scratch = [pltpu.VMEM((2, past, HEADS_PER_GROUP, HEAD_DIM), F32),
               pltpu.VMEM((2, past, HEADS_PER_GROUP, HEAD_DIM), F32),
               pltpu.VMEM((V7X_LANES, GROUP_WIDTH), F32), pltpu.VMEM((V7X_LANES, GROUP_WIDTH), F32)]
    out_shape = [jax.ShapeDtypeStruct((bsz, ls, width), F32)]
    out_specs = [pl.BlockSpec((1, ls, GROUP_WIDTH), lambda b, g, pt: (b, 0, g))]
    if fox:
        nh = b_f.shape[-1]
        in_specs += [pl.BlockSpec((1, ls, V7X_LANES), lambda b, g, pt: (b, 0, fg_cb)),
                     pl.BlockSpec((1, nh), lambda b, g, pt: (0, 0))]
        args += [z3, b_f.reshape(1, nh)]
        caches = [k_cache, v_cache, jnp.swapaxes(lf_cache, 2, 3)]
        scratch.append(pltpu.VMEM((2, n_pages, nh, page), F32))
        out_shape.append(jax.ShapeDtypeStruct((bsz, ls, nh), F32))
        out_specs.append(pl.BlockSpec((1, ls, nh), lambda b, g, pt: (b, 0, 0)))
    else:
        slope_rows = jnp.tile(slopes, ls).reshape(nq, 1)
        in_specs.append(pl.BlockSpec((nq, 1), lambda b, g, pt: (0, 0)))
        args.append(slope_rows)
        caches = [k_cache, v_cache]
    in_specs += [pl.BlockSpec(memory_space=pl.ANY)] * len(caches)
    args += caches
    scratch.append(pltpu.SemaphoreType.DMA((3, 2)))
    grid_spec = pltpu.PrefetchScalarGridSpec(
        num_scalar_prefetch=1, grid=(bsz, n_groups), in_specs=in_specs, out_specs=out_specs,
        scratch_shapes=scratch)
    return pl.pallas_call(
        functools.partial(_decode_kernel, fox=fox, layer=layer, n_pages=n_pages, page=page, n_groups=n_groups,
                          chunk=chunk),
        out_shape=out_shape,
        grid_spec=grid_spec,
        compiler_params=_params("arbitrary", "arbitrary"),
        name="fox_decode" if fox else "moba_decode",
    )(page_table, *args)


def _outproj_kernel(*refs, n_pairs, final_norm):
    pair_refs = refs[:2 * n_pairs]
    h_ref, w_ref = refs[2 * n_pairs:2 * n_pairs + 2]
    rest = refs[2 * n_pairs + 2:]
    nw_ref, o_ref = (rest[0], rest[1]) if final_norm else (None, rest[0])
    acc = h_ref[...]
    off = 0
    for i in range(n_pairs):
        a_ref, g_ref = pair_refs[2 * i], pair_refs[2 * i + 1]
        wd = a_ref.shape[1]
        mixed = (a_ref[...] * _silu(g_ref[...])).astype(BF16)
        acc = acc + jnp.dot(mixed, w_ref[off:off + wd, :], preferred_element_type=F32)
        off += wd
    if final_norm:
        ms = jnp.mean(acc * acc, axis=-1, keepdims=True)
        acc = acc * lax.rsqrt(ms + RMS_EPS) * nw_ref[...]
    o_ref[...] = acc


def _outproj(pairs, h, w_bf16, final_nw=None):
    m, d = h.shape
    tm = _pick_tile(m, 256)
    in_specs, args = [], []
    for a, (g_arr, g_cb) in pairs:
        wd = a.shape[1]
        in_specs += [pl.BlockSpec((tm, wd), lambda i: (i, 0)),
                     pl.BlockSpec((tm, wd), lambda i, g_cb=g_cb: (i, g_cb))]
        args += [a, g_arr]
    in_specs += [pl.BlockSpec((tm, d), lambda i: (i, 0)),
                 pl.BlockSpec(w_bf16.shape, lambda i: (0, 0))]
    args += [h, w_bf16]
    if final_nw is not None:
        in_specs.append(pl.BlockSpec((1, d), lambda i: (0, 0)))
        args.append(final_nw.reshape(1, d))
    return pl.pallas_call(
        functools.partial(_outproj_kernel, n_pairs=len(pairs), final_norm=final_nw is not None),
        out_shape=jax.ShapeDtypeStruct((m, d), F32),
        grid=(m // tm,),
        in_specs=in_specs,
        out_specs=pl.BlockSpec((tm, d), lambda i: (i, 0)),
        compiler_params=_params("parallel"),
        name="outproj",
    )(*args)


def _pad_cols(w, multiple):
    pad = (-w.shape[1]) % multiple
    return jnp.pad(w, ((0, 0), (0, pad))) if pad else w


def kernel(x_prompt, x_sample, state_pool, cache_moba_k, cache_moba_v, cache_fox_k, cache_fox_v, cache_fox_logf,
           page_table, norm_even_w, w_in_even, pool_w, pool_scale, w_out_even, norm_odd_w, w_in_odd, b_forget,
           w_out_odd, norm_final_w):
    bp, lp, d = x_prompt.shape
    bs, ls, _ = x_sample.shape
    assert bp == 1, "the prompt group is one sequence"
    depth = norm_even_w.shape[0] + norm_odd_w.shape[0]
    pw = pool_scale.shape[-1]
    mw = (w_in_even.shape[-1] - 2 * pw) // 4
    moba_heads = mw // HEAD_DIM
    fw = w_out_odd.shape[1]
    fox_heads = fw // HEAD_DIM
    page = cache_moba_k.shape[2]
    past_len = page_table.shape[1] * page
    assert pw == GROUP_WIDTH and mw == GROUP_WIDTH and fw % GROUP_WIDTH == 0
    slopes = jnp.asarray([2.0 ** (-8.0 * (h + 1) / moba_heads) for h in range(moba_heads)], F32)

    hp = x_prompt.reshape(lp, d)
    hs = x_sample.reshape(bs * ls, d)
    pool_p, pool_s, mk_p, mv_p, mk_s, mv_s = [], [], [], [], [], []
    fk_p, fv_p, fl_p, fk_s, fv_s, fl_s = [], [], [], [], [], []
    for layer in range(depth):
        i = layer // 2
        last = layer == depth - 1
        if layer % 2 == 0:
            w_in = w_in_even[i].astype(BF16)
            w_grp = pool_w[i].astype(BF16)
            w_out = w_out_even[i].astype(BF16)
            cb = GROUP_WIDTH // HEAD_DIM
            zp = _inproj(hp, norm_even_w[i], w_in, GROUP_WIDTH)
            zs = _inproj(hs, norm_even_w[i], w_in, GROUP_WIDTH)
            pool_out_p = _pool_prompt(zp, w_grp, pool_scale[i])
            kmean = _kmean(zp, 3, GROUP_WIDTH)
            att_p = _flash(*_attn_prep(zp, 2, 3, 4, moba_heads, slopes=slopes, kmean=kmean))
            zs3 = zs.reshape(bs, ls, -1)
            u_s = zs3[:, :, :pw]
            ext_s = jnp.concatenate([state_pool[i], u_s], axis=1)
            pool_out_s = _pool_sample(ext_s.transpose(1, 0, 2), w_grp, pool_scale[i], past_len)
            pool_out_s = pool_out_s.transpose(1, 0, 2).reshape(bs * ls, pw)
            (att_s,) = _decode(page_table, zs3, 2, 3, 4, cache_moba_k, cache_moba_v, i, slopes=slopes)
            att_s = att_s.reshape(bs * ls, mw)
            u_p = zp[:, :pw]
            ext_p = jnp.concatenate([jnp.zeros((POOL_STATE, pw), F32), u_p[-POOL_STATE:]], axis=0)
            pool_p.append(ext_p[-POOL_STATE:][None])
            pool_s.append(ext_s[:, -POOL_STATE:])
            mk_p.append(zp[:, 3 * pw:4 * pw].reshape(1, lp, moba_heads, HEAD_DIM))
            mv_p.append(zp[:, 4 * pw:5 * pw].reshape(1, lp, moba_heads, HEAD_DIM))
            mk_s.append(zs[:, 3 * pw:4 * pw].reshape(bs, ls, moba_heads, HEAD_DIM))
            mv_s.append(zs[:, 4 * pw:5 * pw].reshape(bs, ls, moba_heads, HEAD_DIM))
            nw = norm_final_w if last else None
            hp = _outproj([(pool_out_p, (zp, 1)), (att_p, (zp, 5))], hp, w_out, nw)
            hs = _outproj([(pool_out_s, (zs, 1)), (att_s, (zs, 5))], hs, w_out, nw)
        else:
            w_in = _pad_cols(w_in_odd[i], 5 * V7X_LANES).astype(BF16)
            w_out = w_out_odd[i].astype(BF16)
            tn = 5 * V7X_LANES
            zp = _inproj(hp, norm_odd_w[i], w_in, tn)
            zs = _inproj(hs, norm_odd_w[i], w_in, tn)
            fg_cb = 4 * fw // V7X_LANES
            cb = fw // HEAD_DIM
            lf_p, c_p = _logf_cumsum(zp, fg_cb, b_forget[i])
            att_p = _flash(*_attn_prep(zp, 0, 1, 2, fox_heads, c=c_p))
            zs3 = zs.reshape(bs, ls, -1)
            gpb = fw // GROUP_WIDTH
            att_s, lf_s = _decode(page_table, zs3, 0, gpb, 2 * gpb, cache_fox_k, cache_fox_v, i,
                                  fg_cb=fg_cb, b_f=b_forget[i], lf_cache=cache_fox_logf)
            att_s = att_s.reshape(bs * ls, fw)
            fk_p.append(zp[:, fw:2 * fw].reshape(1, lp, fox_heads, HEAD_DIM))
            fv_p.append(zp[:, 2 * fw:3 * fw].reshape(1, lp, fox_heads, HEAD_DIM))
            fl_p.append(lf_p.reshape(1, lp, fox_heads).astype(cache_fox_logf.dtype))
            fk_s.append(zs[:, fw:2 * fw].reshape(bs, ls, fox_heads, HEAD_DIM))
            fv_s.append(zs[:, 2 * fw:3 * fw].reshape(bs, ls, fox_heads, HEAD_DIM))
            fl_s.append(lf_s.astype(cache_fox_logf.dtype))
            nw = norm_final_w if last else None
            hp = _outproj([(att_p, (zp, 3))], hp, w_out, nw)
            hs = _outproj([(att_s, (zs, 3))], hs, w_out, nw)
    y_prompt = hp.reshape(bp, lp, d)
    y_sample = hs.reshape(bs, ls, d)
    return (y_prompt, y_sample,
            jnp.stack(pool_p), jnp.stack(pool_s),
            jnp.stack(mk_p), jnp.stack(mv_p), jnp.stack(mk_s), jnp.stack(mv_s),
            jnp.stack(fk_p), jnp.stack(fv_p), jnp.stack(fl_p),
            jnp.stack(fk_s), jnp.stack(fv_s), jnp.stack(fl_s))
```

```python
import functools

import jax
import jax.numpy as jnp
from jax import lax
from jax.experimental import pallas as pl
from jax.experimental.pallas import tpu as pltpu

F32 = jnp.float32
BF16 = jnp.bfloat16

HEAD_DIM = 128
POOL_WINDOWS = (2, 4, 8, 16)
POOL_STATE = max(POOL_WINDOWS) - 1
MOBA_BLOCK = 256
MOBA_TOPK = 3
RMS_EPS = 1e-6

V7X_LANES = 128
V7X_SUBLANES = 8
V7X_VMEM_BYTES = 64 * 1024 * 1024
VMEM_LIMIT_BYTES = V7X_VMEM_BYTES - 8 * 1024 * 1024

HEADS_PER_GROUP = 8
GROUP_WIDTH = HEADS_PER_GROUP * HEAD_DIM
AUX_WIDTH = V7X_LANES
LOG2E = 1.4426950408889634
NEG_MASK = -1e30
NEG_BIG = -2.0 ** 100
M_INIT = -2.0 ** 99
HIGHEST = lax.Precision.HIGHEST
NT_DIMS = (((1,), (1,)), ((), ()))
NN_DIMS = (((1,), (0,)), ((), ()))


def _params(*semantics):
    return pltpu.CompilerParams(dimension_semantics=semantics, vmem_limit_bytes=VMEM_LIMIT_BYTES)


def _pick_tile(n, preferred):
    t = min(n, preferred)
    while n % t:
        t //= 2
    return t


def _log_sigmoid(x):
    return -(jnp.maximum(-x, 0.0) + jnp.log1p(jnp.exp(-jnp.abs(x))))


def _silu(x):
    return x * jax.nn.sigmoid(x)


def _inproj_kernel(*refs, row_chunk, has_tail, w_dims):
    if has_tail:
        x_ref, nw_ref, w_ref, wt_ref, o_ref, ot_ref, xn_ref = refs
    else:
        x_ref, nw_ref, w_ref, o_ref, xn_ref = refs

    def project(w_f32_ref):
        return lax.dot_general(xn_ref[...], w_f32_ref[...].astype(BF16), w_dims, preferred_element_type=F32)

    @pl.when(pl.program_id(1) == 0)
    def _normalize():
        for r in range(0, x_ref.shape[0], row_chunk):
            x = x_ref[r:r + row_chunk, :]
            ms = jnp.mean(x * x, axis=-1, keepdims=True)
            xn_ref[r:r + row_chunk, :] = (x * lax.rsqrt(ms + RMS_EPS) * nw_ref[...]).astype(BF16)
        if has_tail:
            ot_ref[...] = project(wt_ref)

    o_ref[...] = project(w_ref)


def _inproj(x, nw, w, n_main, w_tail=None, w_is_transposed=False):
    m, d = x.shape
    tm = _pick_tile(m, 1024)
    tn = _pick_tile(n_main, 1024)
    if w_is_transposed:
        w_dims, w_block, w_index, tail_block = NT_DIMS, (tn, d), (lambda i, j: (j, 0)), (V7X_LANES, d)
    else:
        w_dims, w_block, w_index, tail_block = NN_DIMS, (d, tn), (lambda i, j: (0, j)), (d, V7X_LANES)
    in_specs = [pl.BlockSpec((tm, d), lambda i, j: (i, 0)),
                pl.BlockSpec((1, d), lambda i, j: (0, 0)),
                pl.BlockSpec(w_block, w_index)]
    args = [x, nw.reshape(1, d), w]
    out_shape = [jax.ShapeDtypeStruct((m, n_main), F32)]
    out_specs = [pl.BlockSpec((tm, tn), lambda i, j: (i, j))]
    if w_tail is not None:
        assert w_tail.shape == tail_block
        in_specs.append(pl.BlockSpec(tail_block, lambda i, j: (0, 0)))
        args.append(w_tail)
        out_shape.append(jax.ShapeDtypeStruct((m, V7X_LANES), F32))
        out_specs.append(pl.BlockSpec((tm, V7X_LANES), lambda i, j: (i, 0)))
    return pl.pallas_call(
        functools.partial(_inproj_kernel, row_chunk=min(tm, 256), has_tail=w_tail is not None, w_dims=w_dims),
        out_shape=out_shape,
        grid=(m // tm, n_main // tn),
        in_specs=in_specs,
        out_specs=out_specs,
        scratch_shapes=[pltpu.VMEM((tm, d), BF16)],
        compiler_params=_params("parallel", "arbitrary"),
        name="inproj",
    )(*args)


def _pool_prompt_kernel(u_ref, prev_ref, w_ref, ps_ref, o_ref, *, halo):
    i = pl.program_id(0)
    tl = u_ref.shape[0]
    gd = w_ref.shape[1]
    cur = u_ref[...]
    prev = jnp.where(i == 0, 0.0, prev_ref[...])
    ext = jnp.concatenate([prev, cur], axis=0)
    pos = i * tl + lax.broadcasted_iota(jnp.int32, (tl, 1), 0)
    for g, w in enumerate(POOL_WINDOWS):
        cols = slice(g * gd, (g + 1) * gd)
        a = ext[:, cols]
        sh = 1
        while sh < w:
            a = a + pltpu.roll(a, sh, axis=0)
            sh *= 2
        cnt = jnp.minimum(w, pos + 1).astype(F32)
        d = a[halo:, :] / cnt - cur[:, cols]
        y = jnp.dot(d.astype(BF16), w_ref[g], preferred_element_type=F32)
        o_ref[:, cols] = y * ps_ref[:, cols]


def _pool_prompt(z, w_grp_bf16, p_scale):
    l = z.shape[0]
    pw = p_scale.shape[-1]
    halo = 16
    tl = _pick_tile(l, 512)
    assert tl % halo == 0 and halo > POOL_STATE
    ng, gd, _ = w_grp_bf16.shape
    return pl.pallas_call(
        functools.partial(_pool_prompt_kernel, halo=halo),
        out_shape=jax.ShapeDtypeStruct((l, pw), F32),
        grid=(l // tl,),
        in_specs=[pl.BlockSpec((tl, pw), lambda i: (i, 0)),
                  pl.BlockSpec((halo, pw), lambda i: (jnp.maximum(i * (tl // halo) - 1, 0), 0)),
                  pl.BlockSpec((ng, gd, gd), lambda i: (0, 0, 0)),
                  pl.BlockSpec((1, pw), lambda i: (0, 0))],
        out_specs=pl.BlockSpec((tl, pw), lambda i: (i, 0)),
        compiler_params=_params("parallel"),
        name="pool_prompt",
    )(z, z, w_grp_bf16, p_scale.reshape(1, pw))


def _pool_sample_kernel(ext_ref, w_ref, ps_ref, o_ref, *, pos0):
    ls = o_ref.shape[0]
    gd = w_ref.shape[1]
    for g, w in enumerate(POOL_WINDOWS):
        cols = slice(g * gd, (g + 1) * gd)
        for i in range(ls):
            row = POOL_STATE + i
            s = ext_ref[row, :, cols]
            for j in range(row - w + 1, row):
                s = s + ext_ref[j, :, cols]
            cnt = float(min(w, pos0 + i + 1))
            d = s / cnt - ext_ref[row, :, cols]
            y = jnp.dot(d.astype(BF16), w_ref[g], preferred_element_type=F32)
            o_ref[i, :, cols] = y * ps_ref[:, cols]


def _pool_sample(ext_t, w_grp_bf16, p_scale, pos0):
    rows, b, pw = ext_t.shape
    ls = rows - POOL_STATE
    return pl.pallas_call(
        functools.partial(_pool_sample_kernel, pos0=pos0),
        out_shape=jax.ShapeDtypeStruct((ls, b, pw), F32),
        compiler_params=pltpu.CompilerParams(vmem_limit_bytes=VMEM_LIMIT_BYTES),
        name="pool_sample",
    )(ext_t, w_grp_bf16, p_scale.reshape(1, pw))


def _kmean_kernel(k_ref, o_ref):
    o_ref[0] = jnp.mean(k_ref[...], axis=0, keepdims=True)


def _kmean(z, col_block, width):
    l = z.shape[0]
    nblk = l // MOBA_BLOCK
    out = pl.pallas_call(
        _kmean_kernel,
        out_shape=jax.ShapeDtypeStruct((nblk, 1, width), F32),
        grid=(nblk,),
        in_specs=[pl.BlockSpec((MOBA_BLOCK, width), lambda n: (n, col_block))],
        out_specs=pl.BlockSpec((1, 1, width), lambda n: (n, 0, 0)),
        compiler_params=_params("parallel"),
        name="moba_kmean",
    )(z)
    return out.reshape(nblk, width)


def _select_topk(gate, cur, topk):
    r, n = gate.shape
    nidx = lax.broadcasted_iota(jnp.int32, (r, n), 1)
    nidx_f = nidx.astype(F32)
    g = jnp.where(nidx < cur, gate, -jnp.inf)
    sel = jnp.zeros((r, n), F32)
    for slot in range(topk):
        mx = jnp.max(g, axis=-1, keepdims=True)
        first = jnp.min(jnp.where(g == mx, nidx_f, float(n)), axis=-1, keepdims=True)
        hit = nidx_f == first
        sel = jnp.where(hit & (cur > slot), 1.0, sel)
        g = jnp.where(hit, -jnp.inf, g)
    return sel


def _split3(x):
    hi = x.astype(BF16).astype(F32)
    rest = x - hi
    mid = rest.astype(BF16).astype(F32)
    lo = (rest - mid).astype(BF16).astype(F32)
    return hi, mid, lo


def _store_head_rows(o_ref, h, x):
    rows, n_heads, width = o_ref.shape
    o_ref.reshape(rows * n_heads, width)[pl.ds(h, rows, stride=n_heads), :] = x


def _attn_prep_kernel(*refs, moba, n_heads, nblk):
    if moba:
        slopes_ref, q_ref, k_ref, v_ref, km_ref, qp_ref, kp_ref, vp_ref, ko_ref, vo_ref = refs
    else:
        q_ref, k_ref, v_ref, c_ref, qp_ref, kp_ref, vp_ref, ko_ref, vo_ref = refs
    tp = q_ref.shape[0]
    pos = pl.program_id(0) * tp + lax.broadcasted_iota(jnp.int32, (tp, 1), 0)
    lane = lax.broadcasted_iota(jnp.int32, (tp, AUX_WIDTH), 1)
    nb = nblk if moba else 0
    ones3 = jnp.where((lane >= nb) & (lane < nb + 3), 1.0, 0.0)
    vaux = jnp.where(lane == 0, 1.0, 0.0).astype(BF16)
    qscale = HEAD_DIM ** -0.5 * LOG2E
    for h in range(n_heads):
        cols = slice(h * HEAD_DIM, (h + 1) * HEAD_DIM)
        q = q_ref[:, cols]
        if moba:
            cur = pos // MOBA_BLOCK
            gate = lax.dot_general(q, km_ref[:, cols], NT_DIMS, precision=HIGHEST, preferred_element_type=F32)
            sel = _select_topk(gate, cur, MOBA_TOPK)
            visible = (sel > 0.0) | (lane == cur)
            qaux = jnp.where(lane < nb, jnp.where(visible, 0.0, NEG_BIG), ones3)
            bias = (slopes_ref[h] * LOG2E) * pos.astype(F32)
            onehot = jnp.where(lane == cur, 1.0, 0.0)
        else:
            hid = lax.broadcasted_iota(jnp.int32, c_ref.shape, 1)
            bias = -LOG2E * jnp.sum(jnp.where(hid == h, c_ref[...], 0.0), axis=-1, keepdims=True)
            qaux = ones3
            onehot = jnp.zeros((tp, AUX_WIDTH), F32)
        hi, mid, lo = _split3(bias)
        kaux = jnp.where(lane == nb, hi, jnp.where(lane == nb + 1, mid, jnp.where(lane == nb + 2, lo, onehot)))
        qp_ref[h] = jnp.concatenate([(q * qscale).astype(BF16), qaux.astype(BF16)], axis=1)
        kp_ref[h] = jnp.concatenate([k_ref[:, cols].astype(BF16), kaux.astype(BF16)], axis=1)
        vp_ref[h] = jnp.concatenate([v_ref[:, cols].astype(BF16), vaux], axis=1)
        _store_head_rows(ko_ref, h, k_ref[:, cols])
        _store_head_rows(vo_ref, h, v_ref[:, cols])


def _attn_prep(z, q_cb, k_cb, v_cb, n_heads, *, slopes=None, kmean=None, c=None):
    moba = kmean is not None
    l = z.shape[0]
    width = n_heads * HEAD_DIM
    tp = _pick_tile(l, 256)
    nblk = l // MOBA_BLOCK
    zspec = lambda cb: pl.BlockSpec((tp, width), lambda i: (i, cb))
    in_specs = [zspec(q_cb), zspec(k_cb), zspec(v_cb)]
    args = [z, z, z]
    if moba:
        assert l % MOBA_BLOCK == 0 and nblk + 3 <= AUX_WIDTH
        km = jnp.pad(kmean, ((0, AUX_WIDTH - nblk), (0, 0)))
        in_specs = [pl.BlockSpec(memory_space=pltpu.SMEM)] + in_specs + [pl.BlockSpec(km.shape, lambda i: (0, 0))]
        args = [slopes] + args + [km]
    else:
        in_specs.append(pl.BlockSpec((tp, c.shape[1]), lambda i: (i, 0)))
        args.append(c)
    out = jax.ShapeDtypeStruct((n_heads, l, HEAD_DIM + AUX_WIDTH), BF16)
    ospec = pl.BlockSpec((n_heads, tp, HEAD_DIM + AUX_WIDTH), lambda i: (0, i, 0))
    cache_out = jax.ShapeDtypeStruct((l, n_heads, HEAD_DIM), F32)
    cache_spec = pl.BlockSpec((tp, n_heads, HEAD_DIM), lambda i: (i, 0, 0))
    return pl.pallas_call(
        functools.partial(_attn_prep_kernel, moba=moba, n_heads=n_heads, nblk=nblk),
        out_shape=(out, out, out, cache_out, cache_out),
        grid=(l // tp,),
        in_specs=in_specs,
        out_specs=(ospec, ospec, ospec, cache_spec, cache_spec),
        compiler_params=_params("parallel"),
        name="moba_prep" if moba else "fox_prep",
    )(*args)


def _flash_kernel(qp_ref, kp_ref, vp_ref, o_ref, m_ref, acc_ref, *, n_split):
    qi = pl.program_id(1)
    tq = qp_ref.shape[1]
    th = tq // n_split
    m_ref[...] = jnp.full(m_ref.shape, M_INIT, F32)
    acc_ref[...] = jnp.zeros(acc_ref.shape, F32)

    def chunk(ki, diagonal):
        start = pl.multiple_of(ki * tq, tq)
        logits = []
        for a in range(n_split):
            n_keys = (a + 1) * th if diagonal else tq
            q = qp_ref[0, a * th:(a + 1) * th, :]
            s = lax.dot_general(q, kp_ref[0, pl.ds(start, n_keys), :], NT_DIMS, preferred_element_type=F32)
            if diagonal:
                row = lax.broadcasted_iota(jnp.int32, (th, n_keys), 0) + a * th
                col = lax.broadcasted_iota(jnp.int32, (th, n_keys), 1)
                s = jnp.where(col <= row, s, NEG_BIG)
            logits.append(s)
        probs, alphas = [], []
        for a in range(n_split):
            rows = slice(a * th, (a + 1) * th)
            m_prev = m_ref[rows, :]
            m_new = jnp.maximum(m_prev, jnp.max(logits[a], axis=-1, keepdims=True))
            alphas.append(jnp.exp2(m_prev - m_new))
            probs.append(jnp.exp2(logits[a] - m_new).astype(BF16))
            m_ref[rows, :] = m_new
        for a in range(n_split):
            rows = slice(a * th, (a + 1) * th)
            n_keys = probs[a].shape[1]
            pv = jnp.dot(probs[a], vp_ref[0, pl.ds(start, n_keys), :], preferred_element_type=F32)
            acc_ref[rows, :] = alphas[a] * acc_ref[rows, :] + pv

    def off_diagonal(ki, carry):
        chunk(ki, False)
        return carry

    lax.fori_loop(0, qi, off_diagonal, None)
    chunk(qi, True)
    acc = acc_ref[...]
    o_ref[...] = acc[:, :HEAD_DIM] / acc[:, HEAD_DIM:HEAD_DIM + 1]


def _flash(qp, kp, vp):
    n_heads, l, aug = qp.shape
    tq = _pick_tile(l, 2048)
    n_split = max(1, tq // 256)
    head_block = pl.BlockSpec((1, l, aug), lambda h, i: (h, 0, 0))
    return pl.pallas_call(
        functools.partial(_flash_kernel, n_split=n_split),
        out_shape=jax.ShapeDtypeStruct((l, n_heads * HEAD_DIM), F32),
        grid=(n_heads, l // tq),
        in_specs=[pl.BlockSpec((1, tq, aug), lambda h, i: (h, i, 0)), head_block, head_block],
        out_specs=pl.BlockSpec((tq, HEAD_DIM), lambda h, i: (i, h)),
        scratch_shapes=[pltpu.VMEM((tq, 1), F32), pltpu.VMEM((tq, aug), F32)],
        compiler_params=_params("parallel", "arbitrary"),
        name="flash",
    )(qp, kp, vp)


def _logf_cumsum_kernel(fg_ref, b_ref, lf_ref, c_ref, carry_ref):
    tc, nh = lf_ref.shape

    @pl.when(pl.program_id(0) == 0)
    def _init():
        carry_ref[...] = jnp.zeros(carry_ref.shape, F32)

    lf = _log_sigmoid(fg_ref[:, :nh] + b_ref[...])
    lf_ref[...] = lf
    row = lax.broadcasted_iota(jnp.int32, (tc, tc), 0)
    col = lax.broadcasted_iota(jnp.int32, (tc, tc), 1)
    tri = (col <= row).astype(F32)
    c = jnp.dot(tri, lf, precision=HIGHEST, preferred_element_type=F32) + carry_ref[...]
    c_ref[...] = c
    carry_ref[...] = c[tc - 1:tc, :]


def _logf_cumsum(fg, b_f):
    l = fg.shape[0]
    nh = b_f.shape[-1]
    tc = _pick_tile(l, 512)
    return pl.pallas_call(
        _logf_cumsum_kernel,
        out_shape=(jax.ShapeDtypeStruct((l, nh), F32), jax.ShapeDtypeStruct((l, nh), F32)),
        grid=(l // tc,),
        in_specs=[pl.BlockSpec((tc, fg.shape[1]), lambda i: (i, 0)),
                  pl.BlockSpec((1, nh), lambda i: (0, 0))],
        out_specs=(pl.BlockSpec((tc, nh), lambda i: (i, 0)), pl.BlockSpec((tc, nh), lambda i: (i, 0))),
        scratch_shapes=[pltpu.VMEM((1, nh), F32)],
        compiler_params=_params("arbitrary"),
        name="logf_cumsum",
    )(fg, b_f.reshape(1, nh))


def _decode_kernel(*refs, fox, layer, n_pages, page, n_groups, chunk):
    if fox:
        (pt_ref, q_ref, kn_ref, vn_ref, fg_ref, bf_ref, kc_hbm, vc_hbm, lfc_hbm,
         o_ref, lf_out_ref, kbuf, vbuf, knbuf, vnbuf, lfbuf, sems) = refs
    else:
        (pt_ref, q_ref, kn_ref, vn_ref, slope_ref, kc_hbm, vc_hbm,
         o_ref, kbuf, vbuf, knbuf, vnbuf, sems) = refs
    b = pl.program_id(0)
    g = pl.program_id(1)
    step = b * n_groups + g
    n_steps = pl.num_programs(0) * n_groups
    slot = step % 2
    ls = q_ref.shape[1]
    nq = ls * HEADS_PER_GROUP
    past = n_pages * page
    n_chunks = past // chunk
    pad_new = V7X_LANES

    def page_copies(bb, gg, sl):
        copies = []
        for j in range(n_pages):
            pg = pt_ref[bb, j]
            rows = pl.ds(j * page, page)
            heads = pl.ds(pl.multiple_of(gg * HEADS_PER_GROUP, HEADS_PER_GROUP), HEADS_PER_GROUP)
            copies.append(pltpu.make_async_copy(kc_hbm.at[layer, pg, :, heads, :], kbuf.at[sl, rows], sems.at[0, sl]))
            copies.append(pltpu.make_async_copy(vc_hbm.at[layer, pg, :, heads, :], vbuf.at[sl, rows], sems.at[1, sl]))
            if fox:
                copies.append(pltpu.make_async_copy(lfc_hbm.at[layer, pg], lfbuf.at[sl, j], sems.at[2, sl]))
        return copies

    def load_wide(buf, ci):
        flat = buf.reshape(2 * past * HEADS_PER_GROUP, HEAD_DIM)
        base = (slot * past + ci * chunk) * HEADS_PER_GROUP
        return jnp.concatenate(
            [flat[pl.ds(base + h, chunk, stride=HEADS_PER_GROUP), :] for h in range(HEADS_PER_GROUP)], axis=1)

    @pl.when(step == 0)
    def _first_fetch():
        for c in page_copies(b, g, slot):
            c.start()

    @pl.when(step + 1 < n_steps)
    def _prefetch_next():
        nxt = step + 1
        for c in page_copies(nxt // n_groups, nxt % n_groups, 1 - slot):
            c.start()

    for c in page_copies(b, g, slot):
        c.wait()

    q = q_ref[0]
    ridx = lax.broadcasted_iota(jnp.int32, (nq, 1), 0)
    r_head = ridx % HEADS_PER_GROUP
    r_query = ridx // HEADS_PER_GROUP
    lane_head = lax.broadcasted_iota(jnp.int32, (1, GROUP_WIDTH), 1) // HEAD_DIM
    head_mask = lane_head == r_head
    q_rep = jnp.concatenate(
        [jnp.broadcast_to(q[i:i + 1, :], (HEADS_PER_GROUP, GROUP_WIDTH)) for i in range(ls)], axis=0)
    q_bd = jnp.where(head_mask, q_rep, 0.0)
    qs_bf16 = (q_bd * HEAD_DIM ** -0.5).astype(BF16)

    s_chunks, ksum_chunks = [], []
    for ci in range(n_chunks):
        kc = load_wide(kbuf, ci)
        s_chunks.append(lax.dot_general(qs_bf16, kc.astype(BF16), NT_DIMS, preferred_element_type=F32))
        if not fox:
            ksum_chunks.append(jnp.sum(kc, axis=0, keepdims=True))
    s_c = jnp.concatenate(s_chunks, axis=1)

    knbuf[...] = jnp.zeros(knbuf.shape, F32)
    vnbuf[...] = jnp.zeros(vnbuf.shape, F32)
    knbuf[0:ls, :] = kn_ref[0]
    vnbuf[0:ls, :] = vn_ref[0]
    kn = knbuf[...]
    vn = vnbuf[...]
    s_n = lax.dot_general(qs_bf16, kn.astype(BF16), NT_DIMS, preferred_element_type=F32)
    new_idx = lax.broadcasted_iota(jnp.int32, (1, pad_new), 1)
    allowed_n = new_idx <= r_query
    kpos = lax.broadcasted_iota(jnp.int32, (1, past), 1)

    if fox:
        nh_all = bf_ref.shape[-1]
        lfn = _log_sigmoid(fg_ref[0][:, :nh_all] + bf_ref[...])
        lf_out_ref[0] = lfn
        hcol = lax.broadcasted_iota(jnp.int32, (nq, nh_all), 1)
        expand = (hcol == g * HEADS_PER_GROUP + r_head).astype(F32)
        lfc = jnp.concatenate([lfbuf[slot, j] for j in range(n_pages)], axis=1)
        x = jnp.dot(expand, lfc, precision=HIGHEST, preferred_element_type=F32)
        jr = lax.broadcasted_iota(jnp.int32, (chunk, chunk), 0)
        jc = lax.broadcasted_iota(jnp.int32, (chunk, chunk), 1)
        upper = (jr > jc).astype(F32)
        run = jnp.zeros((nq, 1), F32)
        sfx = [None] * n_chunks
        for ci in reversed(range(n_chunks)):
            xc = x[:, ci * chunk:(ci + 1) * chunk]
            sfx[ci] = jnp.dot(xc, upper, precision=HIGHEST, preferred_element_type=F32) + run
            run = run + jnp.sum(xc, axis=-1, keepdims=True)
        pre = []
        acc_n = jnp.zeros((nq, 1), F32)
        for i in range(ls):
            acc_n = acc_n + jnp.sum(expand * lfn[i:i + 1, :], axis=-1, keepdims=True)
            pre.append(acc_n)
        n_q = jnp.zeros((nq, 1), F32)
        for i in range(ls):
            n_q = jnp.where(r_query == i, pre[i], n_q)
        pre_new = jnp.concatenate(pre + [jnp.zeros((nq, pad_new - ls), F32)], axis=1)
        s_c = s_c + (jnp.concatenate(sfx, axis=1) + n_q)
        s_n = jnp.where(allowed_n, s_n + (n_q - pre_new), NEG_MASK)
    else:
        nblk = past // MOBA_BLOCK
        assert chunk == MOBA_BLOCK
        kmean = jnp.concatenate(ksum_chunks, axis=0) * (1.0 / MOBA_BLOCK)
        gate = lax.dot_general(q_bd, kmean, NT_DIMS, precision=HIGHEST, preferred_element_type=F32)
        cur = jnp.full((nq, 1), nblk, jnp.int32)
        sel = _select_topk(gate, cur, MOBA_TOPK)
        picked = jnp.concatenate(
            [jnp.broadcast_to(sel[:, n:n + 1], (nq, MOBA_BLOCK)) for n in range(nblk)], axis=1) > 0.0
        slope = slope_ref[...]
        qpos = past + r_query
        s_c = jnp.where(picked, s_c - slope * (qpos - kpos).astype(F32), NEG_MASK)
        s_n = jnp.where(allowed_n, s_n - slope * (r_query - new_idx).astype(F32), NEG_MASK)

    m = jnp.maximum(jnp.max(s_c, axis=-1, keepdims=True), jnp.max(s_n, axis=-1, keepdims=True))
    p_c = jnp.exp(s_c - m)
    p_n = jnp.exp(s_n - m)
    denom = jnp.sum(p_c, axis=-1, keepdims=True) + jnp.sum(p_n, axis=-1, keepdims=True)
    p_c = p_c.astype(BF16)
    out = jnp.dot(p_n.astype(BF16), vn.astype(BF16), preferred_element_type=F32)
    for ci in range(n_chunks):
        vc = load_wide(vbuf, ci)
        out = out + jnp.dot(p_c[:, ci * chunk:(ci + 1) * chunk], vc.astype(BF16), preferred_element_type=F32)
    out = jnp.where(head_mask, out / denom, 0.0)
    for i in range(ls):
        rows = slice(i * HEADS_PER_GROUP, (i + 1) * HEADS_PER_GROUP)
        o_ref[0, i:i + 1, :] = jnp.sum(out[rows, :], axis=0, keepdims=True)


def _decode(page_table, z3, q_cb, k_cb, v_cb, k_cache, v_cache, layer, *, slopes=None,
            fg3=None, b_f=None, lf_cache=None):
    fox = lf_cache is not None
    bsz, ls, _ = z3.shape
    n_pages = page_table.shape[1]
    page = k_cache.shape[2]
    n_heads = k_cache.shape[3]
    width = n_heads * HEAD_DIM
    n_groups = n_heads // HEADS_PER_GROUP
    past = n_pages * page
    chunk = MOBA_BLOCK
    assert past % MOBA_BLOCK == 0 and ls <= V7X_SUBLANES and n_heads % HEADS_PER_GROUP == 0
    nq = ls * HEADS_PER_GROUP

    def zspec(cb):
        return pl.BlockSpec((1, ls, GROUP_WIDTH), lambda b, g, pt: (b, 0, cb + g))

    in_specs = [zspec(q_cb), zspec(k_cb), zspec(v_cb)]
    args = [z3, z3, z3]
    scratch = [pltpu.VMEM((2, past, HEADS_PER_GROUP, HEAD_DIM), F32),
               pltpu.VMEM((2, past, HEADS_PER_GROUP, HEAD_DIM), F32),
               pltpu.VMEM((V7X_LANES, GROUP_WIDTH), F32), pltpu.VMEM((V7X_LANES, GROUP_WIDTH), F32)]
    out_shape = [jax.ShapeDtypeStruct((bsz, ls, width), F32)]
    out_specs = [pl.BlockSpec((1, ls, GROUP_WIDTH), lambda b, g, pt: (b, 0, g))]
    if fox:
        nh = b_f.shape[-1]
        in_specs += [pl.BlockSpec((1, ls, fg3.shape[2]), lambda b, g, pt: (b, 0, 0)),
                     pl.BlockSpec((1, nh), lambda b, g, pt: (0, 0))]
        args += [fg3, b_f.reshape(1, nh)]
        caches = [k_cache, v_cache, jnp.swapaxes(lf_cache, 2, 3)]
        scratch.append(pltpu.VMEM((2, n_pages, nh, page), F32))
        out_shape.append(jax.ShapeDtypeStruct((bsz, ls, nh), F32))
        out_specs.append(pl.BlockSpec((1, ls, nh), lambda b, g, pt: (b, 0, 0)))
    else:
        slope_rows = jnp.tile(slopes, ls).reshape(nq, 1)
        in_specs.append(pl.BlockSpec((nq, 1), lambda b, g, pt: (0, 0)))
        args.append(slope_rows)
        caches = [k_cache, v_cache]
    in_specs += [pl.BlockSpec(memory_space=pl.ANY)] * len(caches)
    args += caches
    scratch.append(pltpu.SemaphoreType.DMA((3, 2)))
    grid_spec = pltpu.PrefetchScalarGridSpec(
        num_scalar_prefetch=1, grid=(bsz, n_groups), in_specs=in_specs, out_specs=out_specs,
        scratch_shapes=scratch)
    return pl.pallas_call(
        functools.partial(_decode_kernel, fox=fox, layer=layer, n_pages=n_pages, page=page, n_groups=n_groups,
                          chunk=chunk),
        out_shape=out_shape,
        grid_spec=grid_spec,
        compiler_params=_params("arbitrary", "arbitrary"),
        name="fox_decode" if fox else "moba_decode",
    )(page_table, *args)


def _outproj_kernel(*refs, n_pairs, final_norm):
    pair_refs = refs[:2 * n_pairs]
    h_ref, w_ref = refs[2 * n_pairs:2 * n_pairs + 2]
    rest = refs[2 * n_pairs + 2:]
    nw_ref, o_ref = (rest[0], rest[1]) if final_norm else (None, rest[0])
    acc = h_ref[...]
    off = 0
    for i in range(n_pairs):
        a_ref, g_ref = pair_refs[2 * i], pair_refs[2 * i + 1]
        wd = a_ref.shape[1]
        mixed = (a_ref[...] * _silu(g_ref[...])).astype(BF16)
        acc = acc + jnp.dot(mixed, w_ref[off:off + wd, :], preferred_element_type=F32)
        off += wd
    if final_norm:
        ms = jnp.mean(acc * acc, axis=-1, keepdims=True)
        acc = acc * lax.rsqrt(ms + RMS_EPS) * nw_ref[...]
    o_ref[...] = acc


def _outproj(pairs, h, w_bf16, final_nw=None):
    m, d = h.shape
    tm = _pick_tile(m, 256)
    in_specs, args = [], []
    for a, (g_arr, g_cb) in pairs:
        wd = a.shape[1]
        in_specs += [pl.BlockSpec((tm, wd), lambda i: (i, 0)),
                     pl.BlockSpec((tm, wd), lambda i, g_cb=g_cb: (i, g_cb))]
        args += [a, g_arr]
    in_specs += [pl.BlockSpec((tm, d), lambda i: (i, 0)),
                 pl.BlockSpec(w_bf16.shape, lambda i: (0, 0))]
    args += [h, w_bf16]
    if final_nw is not None:
        in_specs.append(pl.BlockSpec((1, d), lambda i: (0, 0)))
        args.append(final_nw.reshape(1, d))
    return pl.pallas_call(
        functools.partial(_outproj_kernel, n_pairs=len(pairs), final_norm=final_nw is not None),
        out_shape=jax.ShapeDtypeStruct((m, d), F32),
        grid=(m // tm,),
        in_specs=in_specs,
        out_specs=pl.BlockSpec((tm, d), lambda i: (i, 0)),
        compiler_params=_params("parallel"),
        name="outproj",
    )(*args)


def kernel(x_prompt, x_sample, state_pool, cache_moba_k, cache_moba_v, cache_fox_k, cache_fox_v, cache_fox_logf,
           page_table, norm_even_w, w_in_even, pool_w, pool_scale, w_out_even, norm_odd_w, w_in_odd, b_forget,
           w_out_odd, norm_final_w):
    bp, lp, d = x_prompt.shape
    bs, ls, _ = x_sample.shape
    assert bp == 1, "the prompt group is one sequence"
    depth = norm_even_w.shape[0] + norm_odd_w.shape[0]
    pw = pool_scale.shape[-1]
    mw = (w_in_even.shape[-1] - 2 * pw) // 4
    moba_heads = mw // HEAD_DIM
    fw = w_out_odd.shape[1]
    fox_heads = fw // HEAD_DIM
    page = cache_moba_k.shape[2]
    past_len = page_table.shape[1] * page
    assert pw == GROUP_WIDTH and mw == GROUP_WIDTH and fw % GROUP_WIDTH == 0
    slopes = jnp.asarray([2.0 ** (-8.0 * (h + 1) / moba_heads) for h in range(moba_heads)], F32)

    hp = x_prompt.reshape(lp, d)
    hs = x_sample.reshape(bs * ls, d)
    pool_p, pool_s, mk_p, mv_p, mk_s, mv_s = [], [], [], [], [], []
    fk_p, fv_p, fl_p, fk_s, fv_s, fl_s = [], [], [], [], [], []
    for layer in range(depth):
        i = layer // 2
        last = layer == depth - 1
        if layer % 2 == 0:
            w_grp = pool_w[i].astype(BF16)
            w_out = w_out_even[i].astype(BF16)
            n_in = w_in_even.shape[-1]
            (zp,) = _inproj(hp, norm_even_w[i], w_in_even[i], n_in)
            (zs,) = _inproj(hs, norm_even_w[i], w_in_even[i], n_in)
            pool_out_p = _pool_prompt(zp, w_grp, pool_scale[i])
            kmean = _kmean(zp, 3, GROUP_WIDTH)
            qp, kp, vp, k_p, v_p = _attn_prep(zp, 2, 3, 4, moba_heads, slopes=slopes, kmean=kmean)
            att_p = _flash(qp, kp, vp)
            zs3 = zs.reshape(bs, ls, -1)
            u_s = zs3[:, :, :pw]
            ext_s = jnp.concatenate([state_pool[i], u_s], axis=1)
            pool_out_s = _pool_sample(ext_s.transpose(1, 0, 2), w_grp, pool_scale[i], past_len)
            pool_out_s = pool_out_s.transpose(1, 0, 2).reshape(bs * ls, pw)
            (att_s,) = _decode(page_table, zs3, 2, 3, 4, cache_moba_k, cache_moba_v, i, slopes=slopes)
            att_s = att_s.reshape(bs * ls, mw)
            u_p = zp[:, :pw]
            ext_p = jnp.concatenate([jnp.zeros((POOL_STATE, pw), F32), u_p[-POOL_STATE:]], axis=0)
            pool_p.append(ext_p[-POOL_STATE:][None])
            pool_s.append(ext_s[:, -POOL_STATE:])
            mk_p.append(k_p[None])
            mv_p.append(v_p[None])
            mk_s.append(zs[:, 3 * pw:4 * pw].reshape(bs, ls, moba_heads, HEAD_DIM))
            mv_s.append(zs[:, 4 * pw:5 * pw].reshape(bs, ls, moba_heads, HEAD_DIM))
            nw = norm_final_w if last else None
            hp = _outproj([(pool_out_p, (zp, 1)), (att_p, (zp, 5))], hp, w_out, nw)
            hs = _outproj([(pool_out_s, (zs, 1)), (att_s, (zs, 5))], hs, w_out, nw)
        else:
            w_out = w_out_odd[i].astype(BF16)
            w_t = jnp.swapaxes(w_in_odd[i], 0, 1)
            w_fg = jnp.pad(w_t[4 * fw:], ((0, V7X_LANES - (w_t.shape[0] - 4 * fw)), (0, 0)))
            zp, fg_p = _inproj(hp, norm_odd_w[i], w_t, 4 * fw, w_fg, w_is_transposed=True)
            zs, fg_s = _inproj(hs, norm_odd_w[i], w_t, 4 * fw, w_fg, w_is_transposed=True)
            lf_p, c_p = _logf_cumsum(fg_p, b_forget[i])
            qp, kp, vp, k_p, v_p = _attn_prep(zp, 0, 1, 2, fox_heads, c=c_p)
            att_p = _flash(qp, kp, vp)
            zs3 = zs.reshape(bs, ls, -1)
            gpb = fw // GROUP_WIDTH
            att_s, lf_s = _decode(page_table, zs3, 0, gpb, 2 * gpb, cache_fox_k, cache_fox_v, i,
                                  fg3=fg_s.reshape(bs, ls, -1), b_f=b_forget[i], lf_cache=cache_fox_logf)
            att_s = att_s.reshape(bs * ls, fw)
            fk_p.append(k_p[None])
            fv_p.append(v_p[None])
            fl_p.append(lf_p.reshape(1, lp, fox_heads).astype(cache_fox_logf.dtype))
            fk_s.append(zs[:, fw:2 * fw].reshape(bs, ls, fox_heads, HEAD_DIM))
            fv_s.append(zs[:, 2 * fw:3 * fw].reshape(bs, ls, fox_heads, HEAD_DIM))
            fl_s.append(lf_s.astype(cache_fox_logf.dtype))
            nw = norm_final_w if last else None
            hp = _outproj([(att_p, (zp, 3))], hp, w_out, nw)
            hs = _outproj([(att_s, (zs, 3))], hs, w_out, nw)
    y_prompt = hp.reshape(bp, lp, d)
    y_sample = hs.reshape(bs, ls, d)
    return (y_prompt, y_sample,
            jnp.stack(pool_p), jnp.stack(pool_s),
            jnp.stack(mk_p), jnp.stack(mv_p), jnp.stack(mk_s), jnp.stack(mv_s),
            jnp.stack(fk_p), jnp.stack(fv_p), jnp.stack(fl_p),
            jnp.stack(fk_s), jnp.stack(fv_s), jnp.stack(fl_s))
```

```python
import functools

import jax
import jax.numpy as jnp
from jax import lax
from jax.experimental import pallas as pl
from jax.experimental.pallas import tpu as pltpu

F32 = jnp.float32
BF16 = jnp.bfloat16

HEAD_DIM = 128
POOL_WINDOWS = (2, 4, 8, 16)
POOL_STATE = max(POOL_WINDOWS) - 1
MOBA_BLOCK = 256
MOBA_TOPK = 3
RMS_EPS = 1e-6

V7X_LANES = 128
V7X_SUBLANES = 8
V7X_VMEM_BYTES = 64 * 1024 * 1024
VMEM_LIMIT_BYTES = V7X_VMEM_BYTES - 8 * 1024 * 1024

HEADS_PER_GROUP = 8
GROUP_WIDTH = HEADS_PER_GROUP * HEAD_DIM
AUX_WIDTH = V7X_LANES
LOG2E = 1.4426950408889634
NEG_MASK = -1e30
NEG_BIG = -2.0 ** 100
M_INIT = -2.0 ** 99
HIGHEST = lax.Precision.HIGHEST
NT_DIMS = (((1,), (1,)), ((), ()))
NN_DIMS = (((1,), (0,)), ((), ()))


def _params(*semantics):
    return pltpu.CompilerParams(dimension_semantics=semantics, vmem_limit_bytes=VMEM_LIMIT_BYTES)


def _pick_tile(n, preferred):
    t = min(n, preferred)
    while n % t:
        t //= 2
    return t


def _log_sigmoid(x):
    return -(jnp.maximum(-x, 0.0) + jnp.log1p(jnp.exp(-jnp.abs(x))))


def _silu(x):
    return x * jax.nn.sigmoid(x)


def _inproj_kernel(*refs, row_chunk, has_tail, w_dims):
    if has_tail:
        x_ref, nw_ref, w_ref, wt_ref, o_ref, ot_ref, xn_ref = refs
    else:
        x_ref, nw_ref, w_ref, o_ref, xn_ref = refs

    def project(w_f32_ref):
        return lax.dot_general(xn_ref[...], w_f32_ref[...].astype(BF16), w_dims, preferred_element_type=F32)

    @pl.when(pl.program_id(1) == 0)
    def _normalize():
        for r in range(0, x_ref.shape[0], row_chunk):
            x = x_ref[r:r + row_chunk, :]
            ms = jnp.mean(x * x, axis=-1, keepdims=True)
            xn_ref[r:r + row_chunk, :] = (x * lax.rsqrt(ms + RMS_EPS) * nw_ref[...]).astype(BF16)
        if has_tail:
            ot_ref[...] = project(wt_ref)

    o_ref[...] = project(w_ref)


def _inproj(x, nw, w, n_main, w_tail=None, w_is_transposed=False):
    m, d = x.shape
    tm = _pick_tile(m, 1024)
    tn = _pick_tile(n_main, 1024)
    if w_is_transposed:
        w_dims, w_block, w_index, tail_block = NT_DIMS, (tn, d), (lambda i, j: (j, 0)), (V7X_LANES, d)
    else:
        w_dims, w_block, w_index, tail_block = NN_DIMS, (d, tn), (lambda i, j: (0, j)), (d, V7X_LANES)
    in_specs = [pl.BlockSpec((tm, d), lambda i, j: (i, 0)),
                pl.BlockSpec((1, d), lambda i, j: (0, 0)),
                pl.BlockSpec(w_block, w_index)]
    args = [x, nw.reshape(1, d), w]
    out_shape = [jax.ShapeDtypeStruct((m, n_main), F32)]
    out_specs = [pl.BlockSpec((tm, tn), lambda i, j: (i, j))]
    if w_tail is not None:
        assert w_tail.shape == tail_block
        in_specs.append(pl.BlockSpec(tail_block, lambda i, j: (0, 0)))
        args.append(w_tail)
        out_shape.append(jax.ShapeDtypeStruct((m, V7X_LANES), F32))
        out_specs.append(pl.BlockSpec((tm, V7X_LANES), lambda i, j: (i, 0)))
    return pl.pallas_call(
        functools.partial(_inproj_kernel, row_chunk=min(tm, 256), has_tail=w_tail is not None, w_dims=w_dims),
        out_shape=out_shape,
        grid=(m // tm, n_main // tn),
        in_specs=in_specs,
        out_specs=out_specs,
        scratch_shapes=[pltpu.VMEM((tm, d), BF16)],
        compiler_params=_params("parallel", "arbitrary"),
        name="inproj",
    )(*args)


def _pool_prompt_kernel(u_ref, prev_ref, w_ref, ps_ref, o_ref, *, halo):
    i = pl.program_id(0)
    tl = u_ref.shape[0]
    gd = w_ref.shape[1]
    cur = u_ref[...]
    prev = jnp.where(i == 0, 0.0, prev_ref[...])
    ext = jnp.concatenate([prev, cur], axis=0)
    pos = i * tl + lax.broadcasted_iota(jnp.int32, (tl, 1), 0)
    for g, w in enumerate(POOL_WINDOWS):
        cols = slice(g * gd, (g + 1) * gd)
        a = ext[:, cols]
        sh = 1
        while sh < w:
            a = a + pltpu.roll(a, sh, axis=0)
            sh *= 2
        cnt = jnp.minimum(w, pos + 1).astype(F32)
        d = a[halo:, :] / cnt - cur[:, cols]
        y = jnp.dot(d.astype(BF16), w_ref[g], preferred_element_type=F32)
        o_ref[:, cols] = y * ps_ref[:, cols]


def _pool_prompt(z, w_grp_bf16, p_scale):
    l = z.shape[0]
    pw = p_scale.shape[-1]
    halo = 16
    tl = _pick_tile(l, 512)
    assert tl % halo == 0 and halo > POOL_STATE
    ng, gd, _ = w_grp_bf16.shape
    return pl.pallas_call(
        functools.partial(_pool_prompt_kernel, halo=halo),
        out_shape=jax.ShapeDtypeStruct((l, pw), F32),
        grid=(l // tl,),
        in_specs=[pl.BlockSpec((tl, pw), lambda i: (i, 0)),
                  pl.BlockSpec((halo, pw), lambda i: (jnp.maximum(i * (tl // halo) - 1, 0), 0)),
                  pl.BlockSpec((ng, gd, gd), lambda i: (0, 0, 0)),
                  pl.BlockSpec((1, pw), lambda i: (0, 0))],
        out_specs=pl.BlockSpec((tl, pw), lambda i: (i, 0)),
        compiler_params=_params("parallel"),
        name="pool_prompt",
    )(z, z, w_grp_bf16, p_scale.reshape(1, pw))


def _pool_sample_kernel(ext_ref, w_ref, ps_ref, o_ref, *, pos0):
    ls = o_ref.shape[0]
    gd = w_ref.shape[1]
    for g, w in enumerate(POOL_WINDOWS):
        cols = slice(g * gd, (g + 1) * gd)
        for i in range(ls):
            row = POOL_STATE + i
            s = ext_ref[row, :, cols]
            for j in range(row - w + 1, row):
                s = s + ext_ref[j, :, cols]
            cnt = float(min(w, pos0 + i + 1))
            d = s / cnt - ext_ref[row, :, cols]
            y = jnp.dot(d.astype(BF16), w_ref[g], preferred_element_type=F32)
            o_ref[i, :, cols] = y * ps_ref[:, cols]


def _pool_sample(ext_t, w_grp_bf16, p_scale, pos0):
    rows, b, pw = ext_t.shape
    ls = rows - POOL_STATE
    return pl.pallas_call(
        functools.partial(_pool_sample_kernel, pos0=pos0),
        out_shape=jax.ShapeDtypeStruct((ls, b, pw), F32),
        compiler_params=pltpu.CompilerParams(vmem_limit_bytes=VMEM_LIMIT_BYTES),
        name="pool_sample",
    )(ext_t, w_grp_bf16, p_scale.reshape(1, pw))


def _kmean_kernel(k_ref, o_ref):
    o_ref[0] = jnp.mean(k_ref[...], axis=0, keepdims=True)


def _kmean(z, col_block, width):
    l = z.shape[0]
    nblk = l // MOBA_BLOCK
    out = pl.pallas_call(
        _kmean_kernel,
        out_shape=jax.ShapeDtypeStruct((nblk, 1, width), F32),
        grid=(nblk,),
        in_specs=[pl.BlockSpec((MOBA_BLOCK, width), lambda n: (n, col_block))],
        out_specs=pl.BlockSpec((1, 1, width), lambda n: (n, 0, 0)),
        compiler_params=_params("parallel"),
        name="moba_kmean",
    )(z)
    return out.reshape(nblk, width)


def _select_topk(gate, cur, topk):
    r, n = gate.shape
    nidx = lax.broadcasted_iota(jnp.int32, (r, n), 1)
    nidx_f = nidx.astype(F32)
    g = jnp.where(nidx < cur, gate, -jnp.inf)
    sel = jnp.zeros((r, n), F32)
    for slot in range(topk):
        mx = jnp.max(g, axis=-1, keepdims=True)
        first = jnp.min(jnp.where(g == mx, nidx_f, float(n)), axis=-1, keepdims=True)
        hit = nidx_f == first
        sel = jnp.where(hit & (cur > slot), 1.0, sel)
        g = jnp.where(hit, -jnp.inf, g)
    return sel


def _split3(x):
    hi = x.astype(BF16).astype(F32)
    rest = x - hi
    mid = rest.astype(BF16).astype(F32)
    lo = (rest - mid).astype(BF16).astype(F32)
    return hi, mid, lo


def _store_head_rows(o_ref, h, x):
    rows, n_heads, width = o_ref.shape
    o_ref.reshape(rows * n_heads, width)[pl.ds(h, rows, stride=n_heads), :] = x


def _attn_prep_kernel(*refs, moba, n_heads, nblk):
    if moba:
        slopes_ref, q_ref, k_ref, v_ref, km_ref, qp_ref, kp_ref, vp_ref, ko_ref, vo_ref = refs
    else:
        q_ref, k_ref, v_ref, c_ref, qp_ref, kp_ref, vp_ref, ko_ref, vo_ref = refs
    tp = q_ref.shape[0]
    pos = pl.program_id(0) * tp + lax.broadcasted_iota(jnp.int32, (tp, 1), 0)
    lane = lax.broadcasted_iota(jnp.int32, (tp, AUX_WIDTH), 1)
    nb = nblk if moba else 0
    ones3 = jnp.where((lane >= nb) & (lane < nb + 3), 1.0, 0.0)
    vaux = jnp.where(lane == 0, 1.0, 0.0).astype(BF16)
    qscale = HEAD_DIM ** -0.5 * LOG2E
    for h in range(n_heads):
        cols = slice(h * HEAD_DIM, (h + 1) * HEAD_DIM)
        q = q_ref[:, cols]
        if moba:
            cur = pos // MOBA_BLOCK
            gate = lax.dot_general(q, km_ref[:, cols], NT_DIMS, precision=HIGHEST, preferred_element_type=F32)
            sel = _select_topk(gate, cur, MOBA_TOPK)
            visible = (sel > 0.0) | (lane == cur)
            qaux = jnp.where(lane < nb, jnp.where(visible, 0.0, NEG_BIG), ones3)
            bias = (slopes_ref[h] * LOG2E) * pos.astype(F32)
            onehot = jnp.where(lane == cur, 1.0, 0.0)
        else:
            hid = lax.broadcasted_iota(jnp.int32, c_ref.shape, 1)
            bias = -LOG2E * jnp.sum(jnp.where(hid == h, c_ref[...], 0.0), axis=-1, keepdims=True)
            qaux = ones3
            onehot = jnp.zeros((tp, AUX_WIDTH), F32)
        hi, mid, lo = _split3(bias)
        kaux = jnp.where(lane == nb, hi, jnp.where(lane == nb + 1, mid, jnp.where(lane == nb + 2, lo, onehot)))
        qp_ref[h] = jnp.concatenate([(q * qscale).astype(BF16), qaux.astype(BF16)], axis=1)
        kp_ref[h] = jnp.concatenate([k_ref[:, cols].astype(BF16), kaux.astype(BF16)], axis=1)
        vp_ref[h] = jnp.concatenate([v_ref[:, cols].astype(BF16), vaux], axis=1)
        _store_head_rows(ko_ref, h, k_ref[:, cols])
        _store_head_rows(vo_ref, h, v_ref[:, cols])


def _attn_prep(z, q_cb, k_cb, v_cb, n_heads, *, slopes=None, kmean=None, c=None):
    moba = kmean is not None
    l = z.shape[0]
    width = n_heads * HEAD_DIM
    tp = _pick_tile(l, 256)
    nblk = l // MOBA_BLOCK
    zspec = lambda cb: pl.BlockSpec((tp, width), lambda i: (i, cb))
    in_specs = [zspec(q_cb), zspec(k_cb), zspec(v_cb)]
    args = [z, z, z]
    if moba:
        assert l % MOBA_BLOCK == 0 and nblk + 3 <= AUX_WIDTH
        km = jnp.pad(kmean, ((0, AUX_WIDTH - nblk), (0, 0)))
        in_specs = [pl.BlockSpec(memory_space=pltpu.SMEM)] + in_specs + [pl.BlockSpec(km.shape, lambda i: (0, 0))]
        args = [slopes] + args + [km]
    else:
        in_specs.append(pl.BlockSpec((tp, c.shape[1]), lambda i: (i, 0)))
        args.append(c)
    out = jax.ShapeDtypeStruct((n_heads, l, HEAD_DIM + AUX_WIDTH), BF16)
    ospec = pl.BlockSpec((n_heads, tp, HEAD_DIM + AUX_WIDTH), lambda i: (0, i, 0))
    cache_out = jax.ShapeDtypeStruct((l, n_heads, HEAD_DIM), F32)
    cache_spec = pl.BlockSpec((tp, n_heads, HEAD_DIM), lambda i: (i, 0, 0))
    return pl.pallas_call(
        functools.partial(_attn_prep_kernel, moba=moba, n_heads=n_heads, nblk=nblk),
        out_shape=(out, out, out, cache_out, cache_out),
        grid=(l // tp,),
        in_specs=in_specs,
        out_specs=(ospec, ospec, ospec, cache_spec, cache_spec),
        compiler_params=_params("parallel"),
        name="moba_prep" if moba else "fox_prep",
    )(*args)


def _flash_kernel(qp_ref, kp_ref, vp_ref, o_ref, m_ref, acc_ref, *, n_split):
    qi = pl.program_id(1)
    tq = qp_ref.shape[1]
    th = tq // n_split
    m_ref[...] = jnp.full(m_ref.shape, M_INIT, F32)
    acc_ref[...] = jnp.zeros(acc_ref.shape, F32)

    def chunk(ki, diagonal):
        start = pl.multiple_of(ki * tq, tq)
        logits = []
        for a in range(n_split):
            n_keys = (a + 1) * th if diagonal else tq
            q = qp_ref[0, a * th:(a + 1) * th, :]
            s = lax.dot_general(q, kp_ref[0, pl.ds(start, n_keys), :], NT_DIMS, preferred_element_type=F32)
            if diagonal:
                row = lax.broadcasted_iota(jnp.int32, (th, n_keys), 0) + a * th
                col = lax.broadcasted_iota(jnp.int32, (th, n_keys), 1)
                s = jnp.where(col <= row, s, NEG_BIG)
            logits.append(s)
        probs, alphas = [], []
        for a in range(n_split):
            rows = slice(a * th, (a + 1) * th)
            m_prev = m_ref[rows, :]
            m_new = jnp.maximum(m_prev, jnp.max(logits[a], axis=-1, keepdims=True))
            alphas.append(jnp.exp2(m_prev - m_new))
            probs.append(jnp.exp2(logits[a] - m_new).astype(BF16))
            m_ref[rows, :] = m_new
        for a in range(n_split):
            rows = slice(a * th, (a + 1) * th)
            n_keys = probs[a].shape[1]
            pv = jnp.dot(probs[a], vp_ref[0, pl.ds(start, n_keys), :], preferred_element_type=F32)
            acc_ref[rows, :] = alphas[a] * acc_ref[rows, :] + pv

    def off_diagonal(ki, carry):
        chunk(ki, False)
        return carry

    lax.fori_loop(0, qi, off_diagonal, None)
    chunk(qi, True)
    acc = acc_ref[...]
    o_ref[...] = acc[:, :HEAD_DIM] / acc[:, HEAD_DIM:HEAD_DIM + 1]


def _flash(qp, kp, vp):
    n_heads, l, aug = qp.shape
    tq = _pick_tile(l, 2048)
    n_split = max(1, tq // 256)
    head_block = pl.BlockSpec((1, l, aug), lambda h, i: (h, 0, 0))
    return pl.pallas_call(
        functools.partial(_flash_kernel, n_split=n_split),
        out_shape=jax.ShapeDtypeStruct((l, n_heads * HEAD_DIM), F32),
        grid=(n_heads, l // tq),
        in_specs=[pl.BlockSpec((1, tq, aug), lambda h, i: (h, i, 0)), head_block, head_block],
        out_specs=pl.BlockSpec((tq, HEAD_DIM), lambda h, i: (i, h)),
        scratch_shapes=[pltpu.VMEM((tq, 1), F32), pltpu.VMEM((tq, aug), F32)],
        compiler_params=_params("parallel", "arbitrary"),
        name="flash",
    )(qp, kp, vp)


def _logf_cumsum_kernel(fg_ref, b_ref, lf_ref, c_ref, carry_ref):
    tc, nh = lf_ref.shape

    @pl.when(pl.program_id(0) == 0)
    def _init():
        carry_ref[...] = jnp.zeros(carry_ref.shape, F32)

    lf = _log_sigmoid(fg_ref[:, :nh] + b_ref[...])
    lf_ref[...] = lf
    row = lax.broadcasted_iota(jnp.int32, (tc, tc), 0)
    col = lax.broadcasted_iota(jnp.int32, (tc, tc), 1)
    tri = (col <= row).astype(F32)
    c = jnp.dot(tri, lf, precision=HIGHEST, preferred_element_type=F32) + carry_ref[...]
    c_ref[...] = c
    carry_ref[...] = c[tc - 1:tc, :]


def _logf_cumsum(fg, b_f):
    l = fg.shape[0]
    nh = b_f.shape[-1]
    tc = _pick_tile(l, 512)
    return pl.pallas_call(
        _logf_cumsum_kernel,
        out_shape=(jax.ShapeDtypeStruct((l, nh), F32), jax.ShapeDtypeStruct((l, nh), F32)),
        grid=(l // tc,),
        in_specs=[pl.BlockSpec((tc, fg.shape[1]), lambda i: (i, 0)),
                  pl.BlockSpec((1, nh), lambda i: (0, 0))],
        out_specs=(pl.BlockSpec((tc, nh), lambda i: (i, 0)), pl.BlockSpec((tc, nh), lambda i: (i, 0))),
        scratch_shapes=[pltpu.VMEM((1, nh), F32)],
        compiler_params=_params("arbitrary"),
        name="logf_cumsum",
    )(fg, b_f.reshape(1, nh))


def _decode_kernel(*refs, fox, layer, n_pages, page, n_groups, chunk):
    if fox:
        (pt_ref, q_ref, kn_ref, vn_ref, fg_ref, bf_ref, kc_hbm, vc_hbm, lfc_hbm,
         o_ref, lf_out_ref, kbuf, vbuf, knbuf, vnbuf, lfbuf, sems) = refs
    else:
        (pt_ref, q_ref, kn_ref, vn_ref, slope_ref, kc_hbm, vc_hbm,
         o_ref, kbuf, vbuf, knbuf, vnbuf, sems) = refs
    b = pl.program_id(0)
    g = pl.program_id(1)
    step = b * n_groups + g
    n_steps = pl.num_programs(0) * n_groups
    slot = step % 2
    ls = q_ref.shape[1]
    nq = ls * HEADS_PER_GROUP
    past = n_pages * page
    n_chunks = past // chunk
    pad_new = V7X_LANES

    def page_copies(bb, gg, sl):
        copies = []
        for j in range(n_pages):
            pg = pt_ref[bb, j]
            rows = pl.ds(j * page, page)
            heads = pl.ds(pl.multiple_of(gg * HEADS_PER_GROUP, HEADS_PER_GROUP), HEADS_PER_GROUP)
            copies.append(pltpu.make_async_copy(kc_hbm.at[layer, pg, :, heads, :], kbuf.at[sl, rows], sems.at[0, sl]))
            copies.append(pltpu.make_async_copy(vc_hbm.at[layer, pg, :, heads, :], vbuf.at[sl, rows], sems.at[1, sl]))
            if fox:
                copies.append(pltpu.make_async_copy(lfc_hbm.at[layer, pg], lfbuf.at[sl, j], sems.at[2, sl]))
        return copies

    def load_wide(buf, ci):
        flat = buf.reshape(2 * past * HEADS_PER_GROUP, HEAD_DIM)
        base = (slot * past + ci * chunk) * HEADS_PER_GROUP
        return jnp.concatenate(
            [flat[pl.ds(base + h, chunk, stride=HEADS_PER_GROUP), :] for h in range(HEADS_PER_GROUP)], axis=1)

    @pl.when(step == 0)
    def _first_fetch():
        for c in page_copies(b, g, slot):
            c.start()

    for c in page_copies(b, g, slot):
        c.wait()

    nxt = jnp.minimum(step + 1, n_steps - 1)
    prefetch = page_copies(nxt // n_groups, nxt % n_groups, 1 - slot)
    for c in prefetch:
        c.start()

    q = q_ref[0]
    ridx = lax.broadcasted_iota(jnp.int32, (nq, 1), 0)
    r_head = ridx % HEADS_PER_GROUP
    r_query = ridx // HEADS_PER_GROUP
    lane_head = lax.broadcasted_iota(jnp.int32, (1, GROUP_WIDTH), 1) // HEAD_DIM
    head_mask = lane_head == r_head
    q_rep = jnp.concatenate(
        [jnp.broadcast_to(q[i:i + 1, :], (HEADS_PER_GROUP, GROUP_WIDTH)) for i in range(ls)], axis=0)
    q_bd = jnp.where(head_mask, q_rep, 0.0)
    qs_bf16 = (q_bd * HEAD_DIM ** -0.5).astype(BF16)

    s_chunks, ksum_chunks = [], []
    for ci in range(n_chunks):
        kc = load_wide(kbuf, ci)
        s_chunks.append(lax.dot_general(qs_bf16, kc.astype(BF16), NT_DIMS, preferred_element_type=F32))
        if not fox:
            for r in range(0, chunk, MOBA_BLOCK):
                ksum_chunks.append(jnp.sum(kc[r:r + MOBA_BLOCK, :], axis=0, keepdims=True))
    s_c = jnp.concatenate(s_chunks, axis=1)

    knbuf[...] = jnp.zeros(knbuf.shape, F32)
    vnbuf[...] = jnp.zeros(vnbuf.shape, F32)
    knbuf[0:ls, :] = kn_ref[0]
    vnbuf[0:ls, :] = vn_ref[0]
    kn = knbuf[...]
    vn = vnbuf[...]
    s_n = lax.dot_general(qs_bf16, kn.astype(BF16), NT_DIMS, preferred_element_type=F32)
    new_idx = lax.broadcasted_iota(jnp.int32, (1, pad_new), 1)
    allowed_n = new_idx <= r_query
    kpos = lax.broadcasted_iota(jnp.int32, (1, past), 1)

    if fox:
        nh_all = bf_ref.shape[-1]
        lfn = _log_sigmoid(fg_ref[0][:, :nh_all] + bf_ref[...])
        lf_out_ref[0] = lfn
        hcol = lax.broadcasted_iota(jnp.int32, (nq, nh_all), 1)
        expand = (hcol == g * HEADS_PER_GROUP + r_head).astype(F32)
        lfc = jnp.concatenate([lfbuf[slot, j] for j in range(n_pages)], axis=1)
        x = jnp.dot(expand, lfc, precision=HIGHEST, preferred_element_type=F32)
        jr = lax.broadcasted_iota(jnp.int32, (chunk, chunk), 0)
        jc = lax.broadcasted_iota(jnp.int32, (chunk, chunk), 1)
        upper = (jr > jc).astype(F32)
        run = jnp.zeros((nq, 1), F32)
        sfx = [None] * n_chunks
        for ci in reversed(range(n_chunks)):
            xc = x[:, ci * chunk:(ci + 1) * chunk]
            sfx[ci] = jnp.dot(xc, upper, precision=HIGHEST, preferred_element_type=F32) + run
            run = run + jnp.sum(xc, axis=-1, keepdims=True)
        pre = []
        acc_n = jnp.zeros((nq, 1), F32)
        for i in range(ls):
            acc_n = acc_n + jnp.sum(expand * lfn[i:i + 1, :], axis=-1, keepdims=True)
            pre.append(acc_n)
        n_q = jnp.zeros((nq, 1), F32)
        for i in range(ls):
            n_q = jnp.where(r_query == i, pre[i], n_q)
        pre_new = jnp.concatenate(pre + [jnp.zeros((nq, pad_new - ls), F32)], axis=1)
        s_c = s_c + (jnp.concatenate(sfx, axis=1) + n_q)
        s_n = jnp.where(allowed_n, s_n + (n_q - pre_new), NEG_MASK)
    else:
        nblk = past // MOBA_BLOCK
        kmean =jnp.concatenate(ksum_chunks, axis=0) * (1.0 / MOBA_BLOCK)
        gate = lax.dot_general(q_bd, kmean, NT_DIMS, precision=HIGHEST, preferred_element_type=F32)
        cur = jnp.full((nq, 1), nblk, jnp.int32)
        sel = _select_topk(gate, cur, MOBA_TOPK)
        picked = jnp.concatenate(
            [jnp.broadcast_to(sel[:, n:n + 1], (nq, MOBA_BLOCK)) for n in range(nblk)], axis=1) > 0.0
        slope = slope_ref[...]
        qpos = past + r_query
        s_c = jnp.where(picked, s_c - slope * (qpos - kpos).astype(F32), NEG_MASK)
        s_n = jnp.where(allowed_n, s_n - slope * (r_query - new_idx).astype(F32), NEG_MASK)

    m = jnp.maximum(jnp.max(s_c, axis=-1, keepdims=True), jnp.max(s_n, axis=-1, keepdims=True))
    p_c = jnp.exp(s_c - m)
    p_n = jnp.exp(s_n - m)
    denom = jnp.sum(p_c, axis=-1, keepdims=True) + jnp.sum(p_n, axis=-1, keepdims=True)
    p_c = p_c.astype(BF16)
    out = jnp.dot(p_n.astype(BF16), vn.astype(BF16), preferred_element_type=F32)
    for ci in range(n_chunks):
        vc = load_wide(vbuf, ci)
        out = out + jnp.dot(p_c[:, ci * chunk:(ci + 1) * chunk], vc.astype(BF16), preferred_element_type=F32)
    out = jnp.where(head_mask, out / denom, 0.0)
    for i in range(ls):
        rows = slice(i * HEADS_PER_GROUP, (i + 1) * HEADS_PER_GROUP)
        o_ref[0, i:i + 1, :] = jnp.sum(out[rows, :], axis=0, keepdims=True)

    @pl.when(step == n_steps - 1)
    def _drain_last_prefetch():
        for c in prefetch:
            c.wait()


def _decode(page_table, z3, q_cb, k_cb, v_cb, k_cache, v_cache, layer, *, slopes=None,
            fg3=None, b_f=None, lf_cache=None):
    fox = lf_cache is not None
    bsz, ls, _ = z3.shape
    n_pages = page_table.shape[1]
    page = k_cache.shape[2]
    n_heads = k_cache.shape[3]
    width = n_heads * HEAD_DIM
    n_groups = n_heads // HEADS_PER_GROUP
    past = n_pages * page
    chunk = MOBA_BLOCK
    assert past % chunk == 0 and ls <= V7X_SUBLANES and n_heads % HEADS_PER_GROUP == 0
    nq = ls * HEADS_PER_GROUP

    def zspec(cb):
        return pl.BlockSpec((1, ls, GROUP_WIDTH), lambda b, g, pt: (b, 0, cb + g))

    in_specs = [zspec(q_cb), zspec(k_cb), zspec(v_cb)]
    args = [z3, z3, z3]
    scratch = [pltpu.VMEM((2, past, HEADS_PER_GROUP, HEAD_DIM), F32),
               pltpu.VMEM((2, past, HEADS_PER_GROUP, HEAD_DIM), F32),
               pltpu.VMEM((V7X_LANES, GROUP_WIDTH), F32), pltpu.VMEM((V7X_LANES, GROUP_WIDTH), F32)]
    out_shape = [jax.ShapeDtypeStruct((bsz, ls, width), F32)]
    out_specs = [pl.BlockSpec((1, ls, GROUP_WIDTH), lambda b, g, pt: (b, 0, g))]
    if fox:
        nh = b_f.shape[-1]
        in_specs += [pl.BlockSpec((1, ls, fg3.shape[2]), lambda b, g, pt: (b, 0, 0)),
                     pl.BlockSpec((1, nh), lambda b, g, pt: (0, 0))]
        args += [fg3, b_f.reshape(1, nh)]
        caches = [k_cache, v_cache, jnp.swapaxes(lf_cache, 2, 3)]
        scratch.append(pltpu.VMEM((2, n_pages, nh, page), F32))
        out_shape.append(jax.ShapeDtypeStruct((bsz, ls, nh), F32))
        out_specs.append(pl.BlockSpec((1, ls, nh), lambda b, g, pt: (b, 0, 0)))
    else:
        slope_rows = jnp.tile(slopes, ls).reshape(nq, 1)
        in_specs.append(pl.BlockSpec((nq, 1), lambda b, g, pt: (0, 0)))
        args.append(slope_rows)
        caches = [k_cache, v_cache]
    in_specs += [pl.BlockSpec(memory_space=pl.ANY)] * len(caches)
    args += caches
    scratch.append(pltpu.SemaphoreType.DMA((3, 2)))
    grid_spec = pltpu.PrefetchScalarGridSpec(
        num_scalar_prefetch=1, grid=(bsz, n_groups), in_specs=in_specs, out_specs=out_specs,
        scratch_shapes=scratch)
    return pl.pallas_call(
        functools.partial(_decode_kernel, fox=fox, layer=layer, n_pages=n_pages, page=page, n_groups=n_groups,
                          chunk=chunk),
        out_shape=out_shape,
        grid_spec=grid_spec,
        compiler_params=_params("arbitrary", "arbitrary"),
        name="fox_decode" if fox else "moba_decode",
    )(page_table, *args)


def _outproj_kernel(*refs, n_pairs, final_norm):
    pair_refs = refs[:2 * n_pairs]
    h_ref, w_ref = refs[2 * n_pairs:2 * n_pairs + 2]
    rest = refs[2 * n_pairs + 2:]
    nw_ref, o_ref = (rest[0], rest[1]) if final_norm else (None, rest[0])
    acc = h_ref[...]
    off = 0
    for i in range(n_pairs):
        a_ref, g_ref = pair_refs[2 * i], pair_refs[2 * i + 1]
        wd = a_ref.shape[1]
        mixed = (a_ref[...] * _silu(g_ref[...])).astype(BF16)
        acc = acc + jnp.dot(mixed, w_ref[off:off + wd, :], preferred_element_type=F32)
        off += wd
    if final_norm:
        ms = jnp.mean(acc * acc, axis=-1, keepdims=True)
        acc = acc * lax.rsqrt(ms + RMS_EPS) * nw_ref[...]
    o_ref[...] = acc


def _outproj(pairs, h, w_bf16, final_nw=None):
    m, d = h.shape
    tm = _pick_tile(m, 512)
    in_specs, args = [], []
    for a, (g_arr, g_cb) in pairs:
        wd = a.shape[1]
        in_specs += [pl.BlockSpec((tm, wd), lambda i: (i, 0)),
                     pl.BlockSpec((tm, wd), lambda i, g_cb=g_cb: (i, g_cb))]
        args += [a, g_arr]
    in_specs += [pl.BlockSpec((tm, d), lambda i: (i, 0)),
                 pl.BlockSpec(w_bf16.shape, lambda i: (0, 0))]
    args += [h, w_bf16]
    if final_nw is not None:
        in_specs.append(pl.BlockSpec((1, d), lambda i: (0, 0)))
        args.append(final_nw.reshape(1, d))
    return pl.pallas_call(
        functools.partial(_outproj_kernel, n_pairs=len(pairs), final_norm=final_nw is not None),
        out_shape=jax.ShapeDtypeStruct((m, d), F32),
        grid=(m // tm,),
        in_specs=in_specs,
        out_specs=pl.BlockSpec((tm, d), lambda i: (i, 0)),
        compiler_params=_params("parallel"),
        name="outproj",
    )(*args)


def kernel(x_prompt, x_sample, state_pool, cache_moba_k, cache_moba_v, cache_fox_k, cache_fox_v, cache_fox_logf,
           page_table, norm_even_w, w_in_even, pool_w, pool_scale, w_out_even, norm_odd_w, w_in_odd, b_forget,
           w_out_odd, norm_final_w):
    bp, lp, d = x_prompt.shape
    bs, ls, _ = x_sample.shape
    assert bp == 1, "the prompt group is one sequence"
    depth = norm_even_w.shape[0] + norm_odd_w.shape[0]
    pw = pool_scale.shape[-1]
    mw = (w_in_even.shape[-1] - 2 * pw) // 4
    moba_heads = mw // HEAD_DIM
    fw = w_out_odd.shape[1]
    fox_heads = fw // HEAD_DIM
    page = cache_moba_k.shape[2]
    past_len = page_table.shape[1] * page
    assert pw == GROUP_WIDTH and mw == GROUP_WIDTH and fw % GROUP_WIDTH == 0
    slopes = jnp.asarray([2.0 ** (-8.0 * (h + 1) / moba_heads) for h in range(moba_heads)], F32)

    hp = x_prompt.reshape(lp, d)
    hs = x_sample.reshape(bs * ls, d)
    pool_p, pool_s, mk_p, mv_p, mk_s, mv_s = [], [], [], [], [], []
    fk_p, fv_p, fl_p, fk_s, fv_s, fl_s = [], [], [], [], [], []
    for layer in range(depth):
        i = layer // 2
        last = layer == depth - 1
        if layer % 2 == 0:
            w_grp = pool_w[i].astype(BF16)
            w_out = w_out_even[i].astype(BF16)
            n_in = w_in_even.shape[-1]
            (zp,) = _inproj(hp, norm_even_w[i], w_in_even[i], n_in)
            (zs,) = _inproj(hs, norm_even_w[i], w_in_even[i], n_in)
            pool_out_p = _pool_prompt(zp, w_grp, pool_scale[i])
            kmean = _kmean(zp, 3, GROUP_WIDTH)
            qp, kp, vp, k_p, v_p = _attn_prep(zp, 2, 3, 4, moba_heads, slopes=slopes, kmean=kmean)
            att_p = _flash(qp, kp, vp)
            zs3 = zs.reshape(bs, ls, -1)
            u_s = zs3[:, :, :pw]
            ext_s = jnp.concatenate([state_pool[i], u_s], axis=1)
            pool_out_s = _pool_sample(ext_s.transpose(1, 0, 2), w_grp, pool_scale[i], past_len)
            pool_out_s = pool_out_s.transpose(1, 0, 2).reshape(bs * ls, pw)
            (att_s,) = _decode(page_table, zs3, 2, 3, 4, cache_moba_k, cache_moba_v, i, slopes=slopes)
            att_s = att_s.reshape(bs * ls, mw)
            u_p = zp[:, :pw]
            ext_p = jnp.concatenate([jnp.zeros((POOL_STATE, pw), F32), u_p[-POOL_STATE:]], axis=0)
            pool_p.append(ext_p[-POOL_STATE:][None])
            pool_s.append(ext_s[:, -POOL_STATE:])
            mk_p.append(k_p[None])
            mv_p.append(v_p[None])
            mk_s.append(zs[:, 3 * pw:4 * pw].reshape(bs, ls, moba_heads, HEAD_DIM))
            mv_s.append(zs[:, 4 * pw:5 * pw].reshape(bs, ls, moba_heads, HEAD_DIM))
            nw = norm_final_w if last else None
            hp = _outproj([(pool_out_p, (zp, 1)), (att_p, (zp, 5))], hp, w_out, nw)
            hs = _outproj([(pool_out_s, (zs, 1)), (att_s, (zs, 5))], hs, w_out, nw)
        else:
            w_out = w_out_odd[i].astype(BF16)
            w_t = jnp.swapaxes(w_in_odd[i], 0, 1)
            w_fg = jnp.pad(w_t[4 * fw:], ((0, V7X_LANES - (w_t.shape[0] - 4 * fw)), (0, 0)))
            zp, fg_p = _inproj(hp, norm_odd_w[i], w_t, 4 * fw, w_fg, w_is_transposed=True)
            zs, fg_s = _inproj(hs, norm_odd_w[i], w_t, 4 * fw, w_fg, w_is_transposed=True)
            lf_p, c_p = _logf_cumsum(fg_p, b_forget[i])
            qp, kp, vp, k_p, v_p = _attn_prep(zp, 0, 1, 2, fox_heads, c=c_p)
            att_p = _flash(qp, kp, vp)
            zs3 = zs.reshape(bs, ls, -1)
            gpb = fw // GROUP_WIDTH
            att_s, lf_s = _decode(page_table, zs3, 0, gpb, 2 * gpb, cache_fox_k, cache_fox_v, i,
                                  fg3=fg_s.reshape(bs, ls, -1), b_f=b_forget[i], lf_cache=cache_fox_logf)
            att_s = att_s.reshape(bs * ls, fw)
            fk_p.append(k_p[None])
            fv_p.append(v_p[None])
            fl_p.append(lf_p.reshape(1, lp, fox_heads).astype(cache_fox_logf.dtype))
            fk_s.append(zs[:, fw:2 * fw].reshape(bs, ls, fox_heads, HEAD_DIM))
            fv_s.append(zs[:, 2 * fw:3 * fw].reshape(bs, ls, fox_heads, HEAD_DIM))
            fl_s.append(lf_s.astype(cache_fox_logf.dtype))
            nw = norm_final_w if last else None
            hp = _outproj([(att_p, (zp, 3))], hp, w_out, nw)
            hs = _outproj([(att_s, (zs, 3))], hs, w_out, nw)
    y_prompt = hp.reshape(bp, lp, d)
    y_sample = hs.reshape(bs, ls, d)
    return (y_prompt, y_sample,
            jnp.stack(pool_p), jnp.stack(pool_s),
            jnp.stack(mk_p), jnp.stack(mv_p), jnp.stack(mk_s), jnp.stack(mv_s),
            jnp.stack(fk_p), jnp.stack(fv_p), jnp.stack(fl_p),
            jnp.stack(fk_s), jnp.stack(fv_s), jnp.stack(fl_s))
```

```python
import functools

import jax
import jax.numpy as jnp
from jax import lax
from jax.experimental import pallas as pl
from jax.experimental.pallas import tpu as pltpu

F32 = jnp.float32
BF16 = jnp.bfloat16

HEAD_DIM = 128
POOL_WINDOWS = (2, 4, 8, 16)
POOL_STATE = max(POOL_WINDOWS) - 1
MOBA_BLOCK = 256
MOBA_TOPK = 3
RMS_EPS = 1e-6

V7X_LANES = 128
V7X_SUBLANES = 8
V7X_VMEM_BYTES = 64 * 1024 * 1024
VMEM_LIMIT_BYTES = V7X_VMEM_BYTES - 8 * 1024 * 1024

HEADS_PER_GROUP = 8
GROUP_WIDTH = HEADS_PER_GROUP * HEAD_DIM
AUX_WIDTH = V7X_LANES
LOG2E = 1.4426950408889634
NEG_MASK = -1e30
NEG_BIG = -2.0 ** 100
M_INIT = -2.0 ** 99
HIGHEST = lax.Precision.HIGHEST
NT_DIMS = (((1,), (1,)), ((), ()))
NN_DIMS = (((1,), (0,)), ((), ()))


def _params(*semantics):
    return pltpu.CompilerParams(dimension_semantics=semantics, vmem_limit_bytes=VMEM_LIMIT_BYTES)


def _pick_tile(n, preferred):
    t = min(n, preferred)
    while n % t:
        t //= 2
    return t


def _log_sigmoid(x):
    return -(jnp.maximum(-x, 0.0) + jnp.log1p(jnp.exp(-jnp.abs(x))))


def _silu(x):
    return x * jax.nn.sigmoid(x)


def _inproj_kernel(*refs, row_chunk, has_tail, w_dims):
    if has_tail:
        x_ref, nw_ref, w_ref, wt_ref, o_ref, ot_ref, xn_ref = refs
    else:
        x_ref, nw_ref, w_ref, o_ref, xn_ref = refs

    def project(w_f32_ref):
        return lax.dot_general(xn_ref[...], w_f32_ref[...].astype(BF16), w_dims, preferred_element_type=F32)

    @pl.when(pl.program_id(1) == 0)
    def _normalize():
        for r in range(0, x_ref.shape[0], row_chunk):
            x = x_ref[r:r + row_chunk, :]
            ms = jnp.mean(x * x, axis=-1, keepdims=True)
            xn_ref[r:r + row_chunk, :] = (x * lax.rsqrt(ms + RMS_EPS) * nw_ref[...]).astype(BF16)
        if has_tail:
            ot_ref[...] = project(wt_ref)

    o_ref[...] = project(w_ref)


def _inproj(x, nw, w, n_main, w_tail=None, w_is_transposed=False):
    m, d = x.shape
    tm = _pick_tile(m, 1024)
    tn = _pick_tile(n_main, 1024)
    if w_is_transposed:
        w_dims, w_block, w_index, tail_block = NT_DIMS, (tn, d), (lambda i, j: (j, 0)), (V7X_LANES, d)
    else:
        w_dims, w_block, w_index, tail_block = NN_DIMS, (d, tn), (lambda i, j: (0, j)), (d, V7X_LANES)
    in_specs = [pl.BlockSpec((tm, d), lambda i, j: (i, 0)),
                pl.BlockSpec((1, d), lambda i, j: (0, 0)),
                pl.BlockSpec(w_block, w_index)]
    args = [x, nw.reshape(1, d), w]
    out_shape = [jax.ShapeDtypeStruct((m, n_main), F32)]
    out_specs = [pl.BlockSpec((tm, tn), lambda i, j: (i, j))]
    if w_tail is not None:
        assert w_tail.shape == tail_block
        in_specs.append(pl.BlockSpec(tail_block, lambda i, j: (0, 0)))
        args.append(w_tail)
        out_shape.append(jax.ShapeDtypeStruct((m, V7X_LANES), F32))
        out_specs.append(pl.BlockSpec((tm, V7X_LANES), lambda i, j: (i, 0)))
    return pl.pallas_call(
        functools.partial(_inproj_kernel, row_chunk=min(tm, 256), has_tail=w_tail is not None, w_dims=w_dims),
        out_shape=out_shape,
        grid=(m // tm, n_main // tn),
        in_specs=in_specs,
        out_specs=out_specs,
        scratch_shapes=[pltpu.VMEM((tm, d), BF16)],
        compiler_params=_params("parallel", "arbitrary"),
        name="inproj",
    )(*args)


def _pool_prompt_kernel(u_ref, prev_ref, w_ref, ps_ref, o_ref, *, halo):
    i = pl.program_id(0)
    tl = u_ref.shape[0]
    gd = w_ref.shape[1]
    cur = u_ref[...]
    prev = jnp.where(i == 0, 0.0, prev_ref[...])
    ext = jnp.concatenate([prev, cur], axis=0)
    pos = i * tl + lax.broadcasted_iota(jnp.int32, (tl, 1), 0)
    for g, w in enumerate(POOL_WINDOWS):
        cols = slice(g * gd, (g + 1) * gd)
        a = ext[:, cols]
        sh = 1
        while sh < w:
            a = a + pltpu.roll(a, sh, axis=0)
            sh *= 2
        cnt = jnp.minimum(w, pos + 1).astype(F32)
        d = a[halo:, :] / cnt - cur[:, cols]
        y = jnp.dot(d.astype(BF16), w_ref[g], preferred_element_type=F32)
        o_ref[:, cols] = y * ps_ref[:, cols]


def _pool_prompt(z, w_grp_bf16, p_scale):
    l = z.shape[0]
    pw = p_scale.shape[-1]
    halo = 16
    tl = _pick_tile(l, 512)
    assert tl % halo == 0 and halo > POOL_STATE
    ng, gd, _ = w_grp_bf16.shape
    return pl.pallas_call(
        functools.partial(_pool_prompt_kernel, halo=halo),
        out_shape=jax.ShapeDtypeStruct((l, pw), F32),
        grid=(l // tl,),
        in_specs=[pl.BlockSpec((tl, pw), lambda i: (i, 0)),
                  pl.BlockSpec((halo, pw), lambda i: (jnp.maximum(i * (tl // halo) - 1, 0), 0)),
                  pl.BlockSpec((ng, gd, gd), lambda i: (0, 0, 0)),
                  pl.BlockSpec((1, pw), lambda i: (0, 0))],
        out_specs=pl.BlockSpec((tl, pw), lambda i: (i, 0)),
        compiler_params=_params("parallel"),
        name="pool_prompt",
    )(z, z, w_grp_bf16, p_scale.reshape(1, pw))


def _pool_sample_kernel(ext_ref, w_ref, ps_ref, o_ref, *, pos0):
    ls = o_ref.shape[0]
    gd = w_ref.shape[1]
    for g, w in enumerate(POOL_WINDOWS):
        cols = slice(g * gd, (g + 1) * gd)
        for i in range(ls):
            row = POOL_STATE + i
            s = ext_ref[row, :, cols]
            for j in range(row - w + 1, row):
                s = s + ext_ref[j, :, cols]
            cnt = float(min(w, pos0 + i + 1))
            d = s / cnt - ext_ref[row, :, cols]
            y = jnp.dot(d.astype(BF16), w_ref[g], preferred_element_type=F32)
            o_ref[i, :, cols] = y * ps_ref[:, cols]


def _pool_sample(ext_t, w_grp_bf16, p_scale, pos0):
    rows, b, pw = ext_t.shape
    ls = rows - POOL_STATE
    return pl.pallas_call(
        functools.partial(_pool_sample_kernel, pos0=pos0),
        out_shape=jax.ShapeDtypeStruct((ls, b, pw), F32),
        compiler_params=pltpu.CompilerParams(vmem_limit_bytes=VMEM_LIMIT_BYTES),
        name="pool_sample",
    )(ext_t, w_grp_bf16, p_scale.reshape(1, pw))


def _kmean_kernel(k_ref, o_ref):
    o_ref[0] = jnp.mean(k_ref[...], axis=0, keepdims=True)


def _kmean(z, col_block, width):
    l = z.shape[0]
    nblk = l // MOBA_BLOCK
    out = pl.pallas_call(
        _kmean_kernel,
        out_shape=jax.ShapeDtypeStruct((nblk, 1, width), F32),
        grid=(nblk,),
        in_specs=[pl.BlockSpec((MOBA_BLOCK, width), lambda n: (n, col_block))],
        out_specs=pl.BlockSpec((1, 1, width), lambda n: (n, 0, 0)),
        compiler_params=_params("parallel"),
        name="moba_kmean",
    )(z)
    return out.reshape(nblk, width)


def _select_topk(gate, cur, topk):
    r, n = gate.shape
    nidx = lax.broadcasted_iota(jnp.int32, (r, n), 1)
    nidx_f = nidx.astype(F32)
    g = jnp.where(nidx < cur, gate, -jnp.inf)
    sel = jnp.zeros((r, n), F32)
    for slot in range(topk):
        mx = jnp.max(g, axis=-1, keepdims=True)
        first = jnp.min(jnp.where(g == mx, nidx_f, float(n)), axis=-1, keepdims=True)
        hit = nidx_f == first
        sel = jnp.where(hit & (cur > slot), 1.0, sel)
        g = jnp.where(hit, -jnp.inf, g)
    return sel


def _split3(x):
    hi = x.astype(BF16).astype(F32)
    rest = x - hi
    mid = rest.astype(BF16).astype(F32)
    lo = (rest - mid).astype(BF16).astype(F32)
    return hi, mid, lo


def _store_head_rows(o_ref, h, x):
    rows, n_heads, width = o_ref.shape
    o_ref.reshape(rows * n_heads, width)[pl.ds(h, rows, stride=n_heads), :] = x


def _attn_prep_kernel(*refs, moba, n_heads, nblk):
    if moba:
        slopes_ref, q_ref, k_ref, v_ref, km_ref, qp_ref, kp_ref, vp_ref, ko_ref, vo_ref = refs
    else:
        q_ref, k_ref, v_ref, c_ref, qp_ref, kp_ref, vp_ref, ko_ref, vo_ref = refs
    tp = q_ref.shape[0]
    pos = pl.program_id(0) * tp + lax.broadcasted_iota(jnp.int32, (tp, 1), 0)
    lane = lax.broadcasted_iota(jnp.int32, (tp, AUX_WIDTH), 1)
    nb = nblk if moba else 0
    ones3 = jnp.where((lane >= nb) & (lane < nb + 3), 1.0, 0.0)
    vaux = jnp.where(lane == 0, 1.0, 0.0).astype(BF16)
    qscale = HEAD_DIM ** -0.5 * LOG2E
    for h in range(n_heads):
        cols = slice(h * HEAD_DIM, (h + 1) * HEAD_DIM)
        q = q_ref[:, cols]
        if moba:
            cur = pos // MOBA_BLOCK
            gate = lax.dot_general(q, km_ref[:, cols], NT_DIMS, precision=HIGHEST, preferred_element_type=F32)
            sel = _select_topk(gate, cur, MOBA_TOPK)
            visible = (sel > 0.0) | (lane == cur)
            qaux = jnp.where(lane < nb, jnp.where(visible, 0.0, NEG_BIG), ones3)
            bias = (slopes_ref[h] * LOG2E) * pos.astype(F32)
            onehot = jnp.where(lane == cur, 1.0, 0.0)
        else:
            hid = lax.broadcasted_iota(jnp.int32, c_ref.shape, 1)
            bias = -LOG2E * jnp.sum(jnp.where(hid == h, c_ref[...], 0.0), axis=-1, keepdims=True)
            qaux = ones3
            onehot = jnp.zeros((tp, AUX_WIDTH), F32)
        hi, mid, lo = _split3(bias)
        kaux = jnp.where(lane == nb, hi, jnp.where(lane == nb + 1, mid, jnp.where(lane == nb + 2, lo, onehot)))
        qp_ref[h] = jnp.concatenate([(q * qscale).astype(BF16), qaux.astype(BF16)], axis=1)
        kp_ref[h] = jnp.concatenate([k_ref[:, cols].astype(BF16), kaux.astype(BF16)], axis=1)
        vp_ref[h] = jnp.concatenate([v_ref[:, cols].astype(BF16), vaux], axis=1)
        _store_head_rows(ko_ref, h, k_ref[:, cols])
        _store_head_rows(vo_ref, h, v_ref[:, cols])


def _attn_prep(z, q_cb, k_cb, v_cb, n_heads, *, slopes=None, kmean=None, c=None):
    moba = kmean is not None
    l = z.shape[0]
    width = n_heads * HEAD_DIM
    tp = _pick_tile(l, 256)
    nblk = l // MOBA_BLOCK
    zspec = lambda cb: pl.BlockSpec((tp, width), lambda i: (i, cb))
    in_specs = [zspec(q_cb), zspec(k_cb), zspec(v_cb)]
    args = [z, z, z]
    if moba:
        assert l % MOBA_BLOCK == 0 and nblk + 3 <= AUX_WIDTH
        km = jnp.pad(kmean, ((0, AUX_WIDTH - nblk), (0, 0)))
        in_specs = [pl.BlockSpec(memory_space=pltpu.SMEM)] + in_specs + [pl.BlockSpec(km.shape, lambda i: (0, 0))]
        args = [slopes] + args + [km]
    else:
        in_specs.append(pl.BlockSpec((tp, c.shape[1]), lambda i: (i, 0)))
        args.append(c)
    out = jax.ShapeDtypeStruct((n_heads, l, HEAD_DIM + AUX_WIDTH), BF16)
    ospec = pl.BlockSpec((n_heads, tp, HEAD_DIM + AUX_WIDTH), lambda i: (0, i, 0))
    cache_out = jax.ShapeDtypeStruct((l, n_heads, HEAD_DIM), F32)
    cache_spec = pl.BlockSpec((tp, n_heads, HEAD_DIM), lambda i: (i, 0, 0))
    return pl.pallas_call(
        functools.partial(_attn_prep_kernel, moba=moba, n_heads=n_heads, nblk=nblk),
        out_shape=(out, out, out, cache_out, cache_out),
        grid=(l // tp,),
        in_specs=in_specs,
        out_specs=(ospec, ospec, ospec, cache_spec, cache_spec),
        compiler_params=_params("parallel"),
        name="moba_prep" if moba else "fox_prep",
    )(*args)


def _flash_kernel(qp_ref, kp_ref, vp_ref, o_ref, m_ref, acc_ref, *, n_split):
    qi = pl.program_id(1)
    tq = qp_ref.shape[1]
    th = tq // n_split
    m_ref[...] = jnp.full(m_ref.shape, M_INIT, F32)
    acc_ref[...] = jnp.zeros(acc_ref.shape, F32)

    def chunk(ki, diagonal):
        start = pl.multiple_of(ki * tq, tq)
        logits = []
        for a in range(n_split):
            n_keys = (a + 1) * th if diagonal else tq
            q = qp_ref[0, a * th:(a + 1) * th, :]
            s = lax.dot_general(q, kp_ref[0, pl.ds(start, n_keys), :], NT_DIMS, preferred_element_type=F32)
            if diagonal:
                row = lax.broadcasted_iota(jnp.int32, (th, n_keys), 0) + a * th
                col = lax.broadcasted_iota(jnp.int32, (th, n_keys), 1)
                s = jnp.where(col <= row, s, NEG_BIG)
            logits.append(s)
        probs, alphas = [], []
        for a in range(n_split):
            rows = slice(a * th, (a + 1) * th)
            m_prev = m_ref[rows, :]
            m_new = jnp.maximum(m_prev, jnp.max(logits[a], axis=-1, keepdims=True))
            alphas.append(jnp.exp2(m_prev - m_new))
            probs.append(jnp.exp2(logits[a] - m_new).astype(BF16))
            m_ref[rows, :] = m_new
        for a in range(n_split):
            rows = slice(a * th, (a + 1) * th)
            n_keys = probs[a].shape[1]
            pv = jnp.dot(probs[a], vp_ref[0, pl.ds(start, n_keys), :], preferred_element_type=F32)
            acc_ref[rows, :] = alphas[a] * acc_ref[rows, :] + pv

    def off_diagonal(ki, carry):
        chunk(ki, False)
        return carry

    lax.fori_loop(0, qi, off_diagonal, None)
    chunk(qi, True)
    acc = acc_ref[...]
    o_ref[...] = acc[:, :HEAD_DIM] / acc[:, HEAD_DIM:HEAD_DIM + 1]


def _flash(qp, kp, vp):
    n_heads, l, aug = qp.shape
    tq = _pick_tile(l, 2048)
    n_split = max(1, tq // 256)
    head_block = pl.BlockSpec((1, l, aug), lambda h, i: (h, 0, 0))
    return pl.pallas_call(
        functools.partial(_flash_kernel, n_split=n_split),
        out_shape=jax.ShapeDtypeStruct((l, n_heads * HEAD_DIM), F32),
        grid=(n_heads, l // tq),
        in_specs=[pl.BlockSpec((1, tq, aug), lambda h, i: (h, i, 0)), head_block, head_block],
        out_specs=pl.BlockSpec((tq, HEAD_DIM), lambda h, i: (i, h)),
        scratch_shapes=[pltpu.VMEM((tq, 1), F32), pltpu.VMEM((tq, aug), F32)],
        compiler_params=_params("parallel", "arbitrary"),
        name="flash",
    )(qp, kp, vp)


def _logf_cumsum_kernel(fg_ref, b_ref, lf_ref, c_ref, carry_ref):
    tc, nh = lf_ref.shape

    @pl.when(pl.program_id(0) == 0)
    def _init():
        carry_ref[...] = jnp.zeros(carry_ref.shape, F32)

    lf = _log_sigmoid(fg_ref[:, :nh] + b_ref[...])
    lf_ref[...] = lf
    row = lax.broadcasted_iota(jnp.int32, (tc, tc), 0)
    col = lax.broadcasted_iota(jnp.int32, (tc, tc), 1)
    tri = (col <= row).astype(F32)
    c = jnp.dot(tri, lf, precision=HIGHEST, preferred_element_type=F32) + carry_ref[...]
    c_ref[...] = c
    carry_ref[...] = c[tc - 1:tc, :]


def _logf_cumsum(fg, b_f):
    l = fg.shape[0]
    nh = b_f.shape[-1]
    tc = _pick_tile(l, 512)
    return pl.pallas_call(
        _logf_cumsum_kernel,
        out_shape=(jax.ShapeDtypeStruct((l, nh), F32), jax.ShapeDtypeStruct((l, nh), F32)),
        grid=(l // tc,),
        in_specs=[pl.BlockSpec((tc, fg.shape[1]), lambda i: (i, 0)),
                  pl.BlockSpec((1, nh), lambda i: (0, 0))],
        out_specs=(pl.BlockSpec((tc, nh), lambda i: (i, 0)), pl.BlockSpec((tc, nh), lambda i: (i, 0))),
        scratch_shapes=[pltpu.VMEM((1, nh), F32)],
        compiler_params=_params("arbitrary"),
        name="logf_cumsum",
    )(fg, b_f.reshape(1, nh))


def _decode_kernel(*refs, fox, layer, n_pages, page, n_groups, chunk):
    if fox:
        (pt_ref, q_ref, kn_ref, vn_ref, fg_ref, bf_ref, kc_hbm, vc_hbm, lfc_hbm,
         o_ref, lf_out_ref, kbuf, vbuf, knbuf, vnbuf, lfbuf, sems) = refs
    else:
        (pt_ref, q_ref, kn_ref, vn_ref, slope_ref, kc_hbm, vc_hbm,
         o_ref, kbuf, vbuf, knbuf, vnbuf, sems) = refs
    b = pl.program_id(0)
    g = pl.program_id(1)
    step = b * n_groups + g
    n_steps = pl.num_programs(0) * n_groups
    slot = step % 2
    ls = q_ref.shape[1]
    nq = ls * HEADS_PER_GROUP
    past = n_pages * page
    n_chunks = past // chunk
    pad_new = V7X_LANES

    def page_copies(bb, gg, sl):
        copies = []
        for j in range(n_pages):
            pg = pt_ref[bb, j]
            rows = pl.ds(j * page, page)
            heads = pl.ds(pl.multiple_of(gg * HEADS_PER_GROUP, HEADS_PER_GROUP), HEADS_PER_GROUP)
            copies.append(pltpu.make_async_copy(kc_hbm.at[layer, pg, :, heads, :], kbuf.at[sl, rows], sems.at[0, sl]))
            copies.append(pltpu.make_async_copy(vc_hbm.at[layer, pg, :, heads, :], vbuf.at[sl, rows], sems.at[1, sl]))
            if fox:
                copies.append(pltpu.make_async_copy(lfc_hbm.at[layer, pg], lfbuf.at[sl, j], sems.at[2, sl]))
        return copies

    def load_wide(buf, ci):
        flat = buf.reshape(2 * past * HEADS_PER_GROUP, HEAD_DIM)
        base = (slot * past + ci * chunk) * HEADS_PER_GROUP
        return jnp.concatenate(
            [flat[pl.ds(base + h, chunk, stride=HEADS_PER_GROUP), :] for h in range(HEADS_PER_GROUP)], axis=1)

    @pl.when(step == 0)
    def _first_fetch():
        for c in page_copies(b, g, slot):
            c.start()

    @pl.when(step + 1 < n_steps)
    def _prefetch_next():
        nxt = step + 1
        for c in page_copies(nxt // n_groups, nxt % n_groups, 1 - slot):
            c.start()

    for c in page_copies(b, g, slot):
        c.wait()

    q = q_ref[0]
    ridx = lax.broadcasted_iota(jnp.int32, (nq, 1), 0)
    r_head = ridx % HEADS_PER_GROUP
    r_query = ridx // HEADS_PER_GROUP
    lane_head = lax.broadcasted_iota(jnp.int32, (1, GROUP_WIDTH), 1) // HEAD_DIM
    head_mask = lane_head == r_head
    q_rep = jnp.concatenate(
        [jnp.broadcast_to(q[i:i + 1, :], (HEADS_PER_GROUP, GROUP_WIDTH)) for i in range(ls)], axis=0)
    q_bd = jnp.where(head_mask, q_rep, 0.0)
    qs_bf16 = (q_bd * HEAD_DIM ** -0.5).astype(BF16)

    k_chunks, ksum_chunks = [], []
    for ci in range(n_chunks):
        kc = load_wide(kbuf, ci)
        k_chunks.append(kc.astype(BF16))
        if not fox:
            for r in range(0, chunk, MOBA_BLOCK):
                ksum_chunks.append(jnp.sum(kc[r:r + MOBA_BLOCK, :], axis=0, keepdims=True))
    s_c = lax.dot_general(qs_bf16, jnp.concatenate(k_chunks, axis=0), NT_DIMS,
                          preferred_element_type=F32)

    knbuf[...] = jnp.zeros(knbuf.shape, F32)
    vnbuf[...] = jnp.zeros(vnbuf.shape, F32)
    knbuf[0:ls, :] = kn_ref[0]
    vnbuf[0:ls, :] = vn_ref[0]
    kn = knbuf[...]
    vn = vnbuf[...]
    s_n = lax.dot_general(qs_bf16, kn.astype(BF16), NT_DIMS, preferred_element_type=F32)
    new_idx = lax.broadcasted_iota(jnp.int32, (1, pad_new), 1)
    allowed_n = new_idx <= r_query
    kpos = lax.broadcasted_iota(jnp.int32, (1, past), 1)

    if fox:
        nh_all = bf_ref.shape[-1]
        lfn = _log_sigmoid(fg_ref[0][:, :nh_all] + bf_ref[...])
        lf_out_ref[0] = lfn
        hcol = lax.broadcasted_iota(jnp.int32, (nq, nh_all), 1)
        expand = (hcol == g * HEADS_PER_GROUP + r_head).astype(F32)
        lfc = jnp.concatenate([lfbuf[slot, j] for j in range(n_pages)], axis=1)
        x = jnp.dot(expand, lfc, precision=HIGHEST, preferred_element_type=F32)
        jr = lax.broadcasted_iota(jnp.int32, (chunk, chunk), 0)
        jc = lax.broadcasted_iota(jnp.int32, (chunk, chunk), 1)
        upper = (jr > jc).astype(F32)
        run = jnp.zeros((nq, 1), F32)
        sfx = [None] * n_chunks
        for ci in reversed(range(n_chunks)):
            xc = x[:, ci * chunk:(ci + 1) * chunk]
            sfx[ci] = jnp.dot(xc, upper, precision=HIGHEST, preferred_element_type=F32) + run
            run = run + jnp.sum(xc, axis=-1, keepdims=True)
        pre = []
        acc_n = jnp.zeros((nq, 1), F32)
        for i in range(ls):
            acc_n = acc_n + jnp.sum(expand * lfn[i:i + 1, :], axis=-1, keepdims=True)
            pre.append(acc_n)
        n_q = jnp.zeros((nq, 1), F32)
        for i in range(ls):
            n_q = jnp.where(r_query == i, pre[i], n_q)
        pre_new = jnp.concatenate(pre + [jnp.zeros((nq, pad_new - ls), F32)], axis=1)
        s_c = s_c + (jnp.concatenate(sfx, axis=1) + n_q)
        s_n = jnp.where(allowed_n, s_n + (n_q - pre_new), NEG_MASK)
    else:
        nblk = past // MOBA_BLOCK
        kmean = jnp.concatenate(ksum_chunks, axis=0) * (1.0 / MOBA_BLOCK)
        gate = lax.dot_general(q_bd, kmean, NT_DIMS, precision=HIGHEST, preferred_element_type=F32)
        cur = jnp.full((nq, 1), nblk, jnp.int32)
        sel = _select_topk(gate, cur, MOBA_TOPK)
        picked = jnp.concatenate(
            [jnp.broadcast_to(sel[:, n:n + 1], (nq, MOBA_BLOCK)) for n in range(nblk)], axis=1) > 0.0
        slope = slope_ref[...]
        qpos = past + r_query
        s_c = jnp.where(picked, s_c - slope * (qpos - kpos).astype(F32), NEG_MASK)
        s_n = jnp.where(allowed_n, s_n - slope * (r_query - new_idx).astype(F32), NEG_MASK)

    m = jnp.maximum(jnp.max(s_c, axis=-1, keepdims=True), jnp.max(s_n, axis=-1, keepdims=True))
    p_c = jnp.exp(s_c - m)
    p_n = jnp.exp(s_n - m)
    denom = jnp.sum(p_c, axis=-1, keepdims=True) + jnp.sum(p_n, axis=-1, keepdims=True)
    p_c = p_c.astype(BF16)
    out = jnp.dot(p_n.astype(BF16), vn.astype(BF16), preferred_element_type=F32)
    for ci in range(n_chunks):
        vc = load_wide(vbuf, ci)
        out = out + jnp.dot(p_c[:, ci * chunk:(ci + 1) * chunk], vc.astype(BF16), preferred_element_type=F32)
    out = jnp.where(head_mask, out / denom, 0.0)
    for i in range(ls):
        rows = slice(i * HEADS_PER_GROUP, (i + 1) * HEADS_PER_GROUP)
        o_ref[0, i:i + 1, :] = jnp.sum(out[rows, :], axis=0, keepdims=True)


def _decode(page_table, z3, q_cb, k_cb, v_cb, k_cache, v_cache, layer, *, slopes=None,
            fg3=None, b_f=None, lf_cache=None):
    fox = lf_cache is not None
    bsz, ls, _ = z3.shape
    n_pages = page_table.shape[1]
    page = k_cache.shape[2]
    n_heads = k_cache.shape[3]
    width = n_heads * HEAD_DIM
    n_groups = n_heads // HEADS_PER_GROUP
    past = n_pages * page
    chunk = MOBA_BLOCK
    assert past % chunk == 0 and ls <= V7X_SUBLANES and n_heads % HEADS_PER_GROUP == 0
    nq = ls * HEADS_PER_GROUP

    def zspec(cb):
        return pl.BlockSpec((1, ls, GROUP_WIDTH), lambda b, g, pt: (b, 0, cb + g))

    in_specs = [zspec(q_cb), zspec(k_cb), zspec(v_cb)]
    args = [z3, z3, z3]
    scratch = [pltpu.VMEM((2, past, HEADS_PER_GROUP, HEAD_DIM), F32),
               pltpu.VMEM((2, past, HEADS_PER_GROUP, HEAD_DIM), F32),
               pltpu.VMEM((V7X_LANES, GROUP_WIDTH), F32), pltpu.VMEM((V7X_LANES, GROUP_WIDTH), F32)]
    out_shape = [jax.ShapeDtypeStruct((bsz, ls, width), F32)]
    out_specs = [pl.BlockSpec((1, ls, GROUP_WIDTH), lambda b, g, pt: (b, 0, g))]
    if fox:
        nh = b_f.shape[-1]
        in_specs += [pl.BlockSpec((1, ls, fg3.shape[2]), lambda b, g, pt: (b, 0, 0)),
                     pl.BlockSpec((1, nh), lambda b, g, pt: (0, 0))]
        args += [fg3, b_f.reshape(1, nh)]
        caches = [k_cache, v_cache, jnp.swapaxes(lf_cache, 2, 3)]
        scratch.append(pltpu.VMEM((2, n_pages, nh, page), F32))
        out_shape.append(jax.ShapeDtypeStruct((bsz, ls, nh), F32))
        out_specs.append(pl.BlockSpec((1, ls, nh), lambda b, g, pt: (b, 0, 0)))
    else:
        slope_rows = jnp.tile(slopes, ls).reshape(nq, 1)
        in_specs.append(pl.BlockSpec((nq, 1), lambda b, g, pt: (0, 0)))
        args.append(slope_rows)
        caches = [k_cache, v_cache]
    in_specs += [pl.BlockSpec(memory_space=pl.ANY)] * len(caches)
    args += caches
    scratch.append(pltpu.SemaphoreType.DMA((3, 2)))
    grid_spec = pltpu.PrefetchScalarGridSpec(
        num_scalar_prefetch=1, grid=(bsz, n_groups), in_specs=in_specs, out_specs=out_specs,
        scratch_shapes=scratch)
    return pl.pallas_call(
        functools.partial(_decode_kernel, fox=fox, layer=layer, n_pages=n_pages, page=page, n_groups=n_groups,
                          chunk=chunk),
        out_shape=out_shape,
        grid_spec=grid_spec,
        compiler_params=_params("arbitrary", "arbitrary"),
        name="fox_decode" if fox else "moba_decode",
    )(page_table, *args)


def _outproj_kernel(*refs, n_pairs, final_norm):
    pair_refs = refs[:2 * n_pairs]
    h_ref, w_ref = refs[2 * n_pairs:2 * n_pairs + 2]
    rest = refs[2 * n_pairs + 2:]
    nw_ref, o_ref = (rest[0], rest[1]) if final_norm else (None, rest[0])
    acc = h_ref[...]
    off = 0
    for i in range(n_pairs):
        a_ref, g_ref = pair_refs[2 * i], pair_refs[2 * i + 1]
        wd = a_ref.shape[1]
        mixed = (a_ref[...] * _silu(g_ref[...])).astype(BF16)
        acc = acc + jnp.dot(mixed, w_ref[off:off + wd, :], preferred_element_type=F32)
        off += wd
    if final_norm:
        ms = jnp.mean(acc * acc, axis=-1, keepdims=True)
        acc = acc * lax.rsqrt(ms + RMS_EPS) * nw_ref[...]
    o_ref[...] = acc


def _outproj(pairs, h, w_bf16, final_nw=None):
    m, d = h.shape
    tm = _pick_tile(m, 512)
    in_specs, args = [], []
    for a, (g_arr, g_cb) in pairs:
        wd = a.shape[1]
        in_specs += [pl.BlockSpec((tm, wd), lambda i: (i, 0)),
                     pl.BlockSpec((tm, wd), lambda i, g_cb=g_cb: (i, g_cb))]
        args += [a, g_arr]
    in_specs += [pl.BlockSpec((tm, d), lambda i: (i, 0)),
                 pl.BlockSpec(w_bf16.shape, lambda i: (0, 0))]
    args += [h, w_bf16]
    if final_nw is not None:
        in_specs.append(pl.BlockSpec((1, d), lambda i: (0, 0)))
        args.append(final_nw.reshape(1, d))
    return pl.pallas_call(
        functools.partial(_outproj_kernel, n_pairs=len(pairs), final_norm=final_nw is not None),
        out_shape=jax.ShapeDtypeStruct((m, d), F32),
        grid=(m // tm,),
        in_specs=in_specs,
        out_specs=pl.BlockSpec((tm, d), lambda i: (i, 0)),
        compiler_params=_params("parallel"),
        name="outproj",
    )(*args)


def kernel(x_prompt, x_sample, state_pool, cache_moba_k, cache_moba_v, cache_fox_k, cache_fox_v, cache_fox_logf,
           page_table, norm_even_w, w_in_even, pool_w, pool_scale, w_out_even, norm_odd_w, w_in_odd, b_forget,
           w_out_odd, norm_final_w):
    bp, lp, d = x_prompt.shape
    bs, ls, _ = x_sample.shape
    assert bp == 1, "the prompt group is one sequence"
    depth = norm_even_w.shape[0] + norm_odd_w.shape[0]
    pw = pool_scale.shape[-1]
    mw = (w_in_even.shape[-1] - 2 * pw) // 4
    moba_heads = mw // HEAD_DIM
    fw = w_out_odd.shape[1]
    fox_heads = fw // HEAD_DIM
    page = cache_moba_k.shape[2]
    past_len = page_table.shape[1] * page
    assert pw == GROUP_WIDTH and mw == GROUP_WIDTH and fw % GROUP_WIDTH == 0
    slopes = jnp.asarray([2.0 ** (-8.0 * (h + 1) / moba_heads) for h in range(moba_heads)], F32)

    hp = x_prompt.reshape(lp, d)
    hs = x_sample.reshape(bs * ls, d)
    pool_p, pool_s, mk_p, mv_p, mk_s, mv_s = [], [], [], [], [], []
    fk_p, fv_p, fl_p, fk_s, fv_s, fl_s = [], [], [], [], [], []
    for layer in range(depth):
        i = layer // 2
        last = layer == depth - 1
        if layer % 2 == 0:
            w_grp = pool_w[i].astype(BF16)
            w_out = w_out_even[i].astype(BF16)
            n_in = w_in_even.shape[-1]
            (zp,) = _inproj(hp, norm_even_w[i], w_in_even[i], n_in)
            (zs,) = _inproj(hs, norm_even_w[i], w_in_even[i], n_in)
            pool_out_p = _pool_prompt(zp, w_grp, pool_scale[i])
            kmean = _kmean(zp, 3, GROUP_WIDTH)
            qp, kp, vp, k_p, v_p = _attn_prep(zp, 2, 3, 4, moba_heads, slopes=slopes, kmean=kmean)
            att_p = _flash(qp, kp, vp)
            zs3 = zs.reshape(bs, ls, -1)
            u_s = zs3[:, :, :pw]
            ext_s = jnp.concatenate([state_pool[i], u_s], axis=1)
            pool_out_s = _pool_sample(ext_s.transpose(1, 0, 2), w_grp, pool_scale[i], past_len)
            pool_out_s = pool_out_s.transpose(1, 0, 2).reshape(bs * ls, pw)
            (att_s,) = _decode(page_table, zs3, 2, 3, 4, cache_moba_k, cache_moba_v, i, slopes=slopes)
            att_s = att_s.reshape(bs * ls, mw)
            u_p = zp[:, :pw]
            ext_p = jnp.concatenate([jnp.zeros((POOL_STATE, pw), F32), u_p[-POOL_STATE:]], axis=0)
            pool_p.append(ext_p[-POOL_STATE:][None])
            pool_s.append(ext_s[:, -POOL_STATE:])
            mk_p.append(k_p[None])
            mv_p.append(v_p[None])
            mk_s.append(zs[:, 3 * pw:4 * pw].reshape(bs, ls, moba_heads, HEAD_DIM))
            mv_s.append(zs[:, 4 * pw:5 * pw].reshape(bs, ls, moba_heads, HEAD_DIM))
            nw = norm_final_w if last else None
            hp = _outproj([(pool_out_p, (zp, 1)), (att_p, (zp, 5))], hp, w_out, nw)
            hs = _outproj([(pool_out_s, (zs, 1)), (att_s, (zs, 5))], hs, w_out, nw)
        else:
            w_out = w_out_odd[i].astype(BF16)
            w_t = jnp.swapaxes(w_in_odd[i], 0, 1)
            w_fg = jnp.pad(w_t[4 * fw:], ((0, V7X_LANES - (w_t.shape[0] - 4 * fw)), (0, 0)))
            zp, fg_p = _inproj(hp, norm_odd_w[i], w_t, 4 * fw, w_fg, w_is_transposed=True)
            zs, fg_s = _inproj(hs, norm_odd_w[i], w_t, 4 * fw, w_fg, w_is_transposed=True)
            lf_p, c_p = _logf_cumsum(fg_p, b_forget[i])
            qp, kp, vp, k_p, v_p = _attn_prep(zp, 0, 1, 2, fox_heads, c=c_p)
            att_p = _flash(qp, kp, vp)
            zs3 = zs.reshape(bs, ls, -1)
            gpb = fw // GROUP_WIDTH
            att_s, lf_s = _decode(page_table, zs3, 0, gpb, 2 * gpb, cache_fox_k, cache_fox_v, i,
                                  fg3=fg_s.reshape(bs, ls, -1), b_f=b_forget[i], lf_cache=cache_fox_logf)
            att_s = att_s.reshape(bs * ls, fw)
            fk_p.append(k_p[None])
            fv_p.append(v_p[None])
            fl_p.append(lf_p.reshape(1, lp, fox_heads).astype(cache_fox_logf.dtype))
            fk_s.append(zs[:, fw:2 * fw].reshape(bs, ls, fox_heads, HEAD_DIM))
            fv_s.append(zs[:, 2 * fw:3 * fw].reshape(bs, ls, fox_heads, HEAD_DIM))
            fl_s.append(lf_s.astype(cache_fox_logf.dtype))
            nw = norm_final_w if last else None
            hp = _outproj([(att_p, (zp, 3))], hp, w_out, nw)
            hs = _outproj([(att_s, (zs, 3))], hs, w_out, nw)
    y_prompt = hp.reshape(bp, lp, d)
    y_sample = hs.reshape(bs, ls, d)
    return (y_prompt, y_sample,
            jnp.stack(pool_p), jnp.stack(pool_s),
            jnp.stack(mk_p), jnp.stack(mv_p), jnp.stack(mk_s), jnp.stack(mv_s),
            jnp.stack(fk_p), jnp.stack(fv_p), jnp.stack(fl_p),
            jnp.stack(fk_s), jnp.stack(fv_s), jnp.stack(fl_s))
```
